```python
import math
import jax
import jax.numpy as jnp
from jax import lax
import numpy as np

D_MODEL = 2048
BATCH = 1
SEQ = 16384
DEPTH = 2

GRID_W = 64
Q_BLOCK = 128
HEAD_DIM = 128
NORM_EPS = 1e-6

A_HEADS = 4
A_QK_DIM = HEAD_DIM // 2
A_V_DIM = HEAD_DIM
A_WIDTH = A_HEADS * A_V_DIM
B_HEADS = 8
B_KV_HEADS = 2
B_WIDTH = B_HEADS * HEAD_DIM
ROPE_THETA = 10000.0
C_HEADS = 4
C_WIDTH = C_HEADS * HEAD_DIM
NA_ROWS = 8
NA_COLS = 16
T5_BUCKETS = 32
T5_MAX_DIST = 128
N_BRANCH = 3
SPLIT_SIZES = (A_HEADS * 2 * A_QK_DIM, A_HEADS * 2 * A_QK_DIM, A_WIDTH,
               B_WIDTH, B_KV_HEADS * HEAD_DIM, B_KV_HEADS * HEAD_DIM,
               C_WIDTH, C_WIDTH, C_WIDTH,
               N_BRANCH * D_MODEL)
IN_WIDTH = 10752
FFN_DIM = 5632
N_EXPERTS = 8
TOP_K = 2
EXPERT_DIM = 7168
MOE_BLOCK = 128
N_DENSE = (DEPTH + 1) // 2
N_MOE = DEPTH // 2

kernel_name = "hybrid_diffattn_gqa_natten_moe_encoder"


def _rmsnorm(x, g):
    xf = x.astype(jnp.float32)
    y = xf * lax.rsqrt(jnp.mean(xf * xf, axis=-1, keepdims=True) + NORM_EPS)
    return (y * g.astype(jnp.float32)).astype(x.dtype)


def _t5_bucket(rel):
    nb = T5_BUCKETS // 2
    max_exact = nb // 2
    ret = jnp.where(rel > 0, nb, 0)
    n = jnp.abs(rel)
    nf = jnp.maximum(n, 1).astype(jnp.float32)
    large = max_exact + (jnp.log(nf / max_exact) / math.log(T5_MAX_DIST / max_exact)
                         * (nb - max_exact)).astype(jnp.int32)
    large = jnp.minimum(large, nb - 1)
    return ret + jnp.where(n < max_exact, n, large)


def _diff_attention(q, k, v, lam, lam_init, subln_g, t5_table):
    bn, s_len = q.shape[0], q.shape[1]
    nblk = s_len // Q_BLOCK
    scale = A_QK_DIM ** -0.5
    qh = jnp.transpose(q, (0, 2, 3, 1, 4))
    kh = jnp.transpose(k, (0, 2, 3, 1, 4))
    vh = jnp.transpose(v, (0, 2, 1, 3))
    qblk = jnp.moveaxis(qh.reshape(bn, A_HEADS, 2, nblk, Q_BLOCK, A_QK_DIM), 3, 0)
    kpos = jnp.arange(s_len, dtype=jnp.int32)

    def block(args):
        qb, i = args
        qpos = i * Q_BLOCK + jnp.arange(Q_BLOCK, dtype=jnp.int32)
        bias = t5_table[_t5_bucket(kpos[None, :] - qpos[:, None])]
        bias = jnp.transpose(bias, (2, 0, 1)).astype(jnp.float32)
        sc = jnp.einsum('bhmqd,bhmkd->bhmqk', qb, kh).astype(jnp.float32) * scale
        p = jax.nn.softmax(sc + bias[None, :, None], axis=-1)
        w = (p[:, :, 0] - lam * p[:, :, 1]).astype(vh.dtype)
        return jnp.einsum('bhqk,bhkd->bhqd', w, vh)

    o = lax.map(block, (qblk, jnp.arange(nblk, dtype=jnp.int32)))
    o = jnp.transpose(o, (1, 0, 3, 2, 4)).reshape(bn, s_len, A_HEADS, A_V_DIM)
    o = _rmsnorm(o, subln_g) * (1.0 - lam_init)
    return o.reshape(bn, s_len, A_WIDTH)


def _rope_segment(xs, ang):
    c = jnp.cos(ang)[None, :, None, :]
    s = jnp.sin(ang)[None, :, None, :]
    x1, x2 = jnp.split(xs, 2, axis=-1)
    return jnp.concatenate([x1 * c - x2 * s, x2 * c + x1 * s], axis=-1)


def _axial_rope(x, row, col):
    half = HEAD_DIM // 2
    inv = ROPE_THETA ** (-jnp.arange(0, half, 2, dtype=jnp.float32) / half)
    ang_r = row.astype(jnp.float32)[:, None] * inv[None, :]
    ang_c = col.astype(jnp.float32)[:, None] * inv[None, :]
    xf = x.astype(jnp.float32)
    out = jnp.concatenate([_rope_segment(xf[..., :half], ang_r),
                           _rope_segment(xf[..., half:], ang_c)], axis=-1)
    return out.astype(x.dtype)


def _gqa_axial(q, k, v, q_gain, k_gain):
    bn, s_len = q.shape[0], q.shape[1]
    nblk = s_len // Q_BLOCK
    group = B_HEADS // B_KV_HEADS
    scale = HEAD_DIM ** -0.5
    pos = jnp.arange(s_len, dtype=jnp.int32)
    row, col = pos // GRID_W, pos % GRID_W
    q = _axial_rope(_rmsnorm(q, q_gain), row, col)
    k = _axial_rope(_rmsnorm(k, k_gain), row, col)
    qh = jnp.transpose(q.reshape(bn, s_len, B_KV_HEADS, group, HEAD_DIM), (0, 2, 3, 1, 4))
    kh = jnp.transpose(k, (0, 2, 1, 3))
    vh = jnp.transpose(v, (0, 2, 1, 3))
    qblk = jnp.moveaxis(qh.reshape(bn, B_KV_HEADS, group, nblk, Q_BLOCK, HEAD_DIM), 3, 0)

    def block(qb):
        sc = jnp.einsum('bkgqd,bksd->bkgqs', qb, kh).astype(jnp.float32) * scale
        p = jax.nn.softmax(sc, axis=-1).astype(vh.dtype)
        return jnp.einsum('bkgqs,bksd->bkgqd', p, vh)

    o = lax.map(block, qblk)
    return jnp.transpose(o, (1, 0, 4, 2, 3, 5)).reshape(bn, s_len, B_WIDTH)


def _neighbourhood_attention(q, k, v, rpb):
    bn, s_len = q.shape[0], q.shape[1]
    rows = s_len // GRID_W
    kh_ = min(NA_ROWS, rows)
    kw = NA_COLS
    scale = HEAD_DIM ** -0.5

    def to_grid(t):
        return jnp.transpose(t.reshape(bn, rows, GRID_W, C_HEADS, HEAD_DIM), (0, 3, 1, 2, 4))

    qg, kg, vg = to_grid(q), to_grid(k), to_grid(v)
    r = jnp.arange(rows, dtype=jnp.int32)
    rs = jnp.clip(r - kh_ // 2, 0, rows - kh_)
    ridx = rs[:, None] + jnp.arange(kh_, dtype=jnp.int32)[None, :]
    c = jnp.arange(GRID_W, dtype=jnp.int32)
    cs = jnp.clip(c - kw // 2, 0, GRID_W - kw)
    valid = (c[None, :] >= cs[:, None]) & (c[None, :] < cs[:, None] + kw)
    kb = kg[:, :, ridx]
    vb = vg[:, :, ridx]
    sc = jnp.einsum('bhrqd,bhrikd->bhrqik', qg, kb).astype(jnp.float32) * scale
    roff = ridx - r[:, None] + (NA_ROWS - 1)
    coff = jnp.clip(c[None, :] - c[:, None] + (NA_COLS - 1), 0, 2 * NA_COLS - 2)
    bias = rpb[:, roff[:, None, :, None], coff[None, :, None, :]]
    sc = sc + bias[None].astype(jnp.float32)
    sc = jnp.where(valid[:, None, :], sc, -1e30)
    p = jax.nn.softmax(sc, axis=(-2, -1)).astype(vb.dtype)
    o = jnp.einsum('bhrqik,bhrikd->bhrqd', p, vb)
    return jnp.transpose(o, (0, 2, 3, 1, 4)).reshape(bn, s_len, C_WIDTH)


def _swiglu(h, w_gate, w_up, w_down):
    return (jax.nn.silu(h @ w_gate) * (h @ w_up)) @ w_down


def _moe_swiglu(h, w_router, w_gate, w_up, w_down):
    bn, s_len, d = h.shape
    n_tok = bn * s_len
    hf = h.reshape(n_tok, d)
    logits = (hf @ w_router).astype(jnp.float32)
    top_v, top_i = lax.top_k(logits, TOP_K)
    gates = jax.nn.softmax(top_v, axis=-1)
    n_assign = n_tok * TOP_K
    e_flat = top_i.reshape(-1).astype(jnp.int32)
    tok_flat = jnp.repeat(jnp.arange(n_tok, dtype=jnp.int32), TOP_K)
    g_flat = gates.reshape(-1)
    order = jnp.argsort(e_flat)
    e_sorted, tok_sorted, g_sorted = e_flat[order], tok_flat[order], g_flat[order]
    counts = jnp.bincount(e_flat, length=N_EXPERTS).astype(jnp.int32)
    starts = jnp.cumsum(counts) - counts
    pcounts = (counts + MOE_BLOCK - 1) // MOE_BLOCK * MOE_BLOCK
    pends = jnp.cumsum(pcounts)
    pstarts = pends - pcounts
    cap = (n_assign + MOE_BLOCK - 1) // MOE_BLOCK * MOE_BLOCK + N_EXPERTS * MOE_BLOCK
    rank = jnp.arange(n_assign, dtype=jnp.int32) - starts[e_sorted]
    dest = pstarts[e_sorted] + rank
    buf_tok = jnp.zeros((cap,), jnp.int32).at[dest].set(tok_sorted)
    buf_g = jnp.zeros((cap,), jnp.float32).at[dest].set(g_sorted)
    nblk = cap // MOE_BLOCK
    blk_e = jnp.minimum(jnp.searchsorted(pends, jnp.arange(nblk, dtype=jnp.int32) * MOE_BLOCK,
                                         side='right'), N_EXPERTS - 1).astype(jnp.int32)
    xb = hf[buf_tok].reshape(nblk, MOE_BLOCK, d)

    def expert_block(args):
        xe, e = args
        return _swiglu(xe, w_gate[e], w_up[e], w_down[e])

    yb = lax.map(expert_block, (xb, blk_e)).reshape(cap, d)
    y = jax.ops.segment_sum(yb * buf_g[:, None].astype(yb.dtype), buf_tok, num_segments=n_tok)
    return y.reshape(bn, s_len, d)


def setup_inputs(seed: int = 0) -> dict:
    key = jax.random.key(seed)
    ks = jax.random.split(key, 24)
    f32 = jnp.float32
    d = D_MODEL

    def nrm(k, shape, scale):
        return jax.random.normal(k, shape, f32) * scale

    return {
        "x": nrm(ks[0], (BATCH, SEQ, d), 1.0),
        "w_in": nrm(ks[1], (DEPTH, d, IN_WIDTH), d ** -0.5),
        "w_branch_a": nrm(ks[2], (DEPTH, A_WIDTH, d), A_WIDTH ** -0.5),
        "w_branch_b": nrm(ks[3], (DEPTH, B_WIDTH, d), B_WIDTH ** -0.5),
        "w_branch_c": nrm(ks[4], (DEPTH, C_WIDTH, d), C_WIDTH ** -0.5),
        "w_out": nrm(ks[5], (DEPTH, d, d), d ** -0.5),
        "norm_mix": 1.0 + nrm(ks[6], (DEPTH, d), 0.02),
        "norm_ffn": 1.0 + nrm(ks[7], (DEPTH, d), 0.02),
        "norm_final": 1.0 + nrm(ks[8], (d,), 0.02),
        "t5_bias": nrm(ks[9], (T5_BUCKETS, A_HEADS), 0.5),
        "diff_lambda": nrm(ks[10], (DEPTH, 4, A_QK_DIM), 0.1),
        "diff_subln": 1.0 + nrm(ks[11], (DEPTH, A_V_DIM), 0.02),
        "qk_norm_b": 1.0 + nrm(ks[12], (DEPTH, 2, HEAD_DIM), 0.02),
        "na_rpb": nrm(ks[13], (DEPTH, C_HEADS, 2 * NA_ROWS - 1, 2 * NA_COLS - 1), 0.5),
        "ffn_gate": nrm(ks[14], (N_DENSE, d, FFN_DIM), d ** -0.5),
        "ffn_up": nrm(ks[15], (N_DENSE, d, FFN_DIM), d ** -0.5),
        "ffn_down": nrm(ks[16], (N_DENSE, FFN_DIM, d), FFN_DIM ** -0.5),
        "moe_router": nrm(ks[17], (N_MOE, d, N_EXPERTS), d ** -0.5),
        "moe_gate": nrm(ks[18], (N_MOE, N_EXPERTS, d, EXPERT_DIM), d ** -0.5),
        "moe_up": nrm(ks[19], (N_MOE, N_EXPERTS, d, EXPERT_DIM), d ** -0.5),
        "moe_down": nrm(ks[20], (N_MOE, N_EXPERTS, EXPERT_DIM, d), EXPERT_DIM ** -0.5),
    }


def reference(x, w_in, w_branch_a, w_branch_b, w_branch_c, w_out, norm_mix, norm_ffn,
              norm_final, t5_bias, diff_lambda, diff_subln, qk_norm_b, na_rpb,
              ffn_gate, ffn_up, ffn_down, moe_router, moe_gate, moe_up, moe_down):
    bn, s_len, d = x.shape
    split_points = [int(v) for v in np.cumsum(SPLIT_SIZES)[:-1]]
    for l in range(DEPTH):
        h = _rmsnorm(x, norm_mix[l])
        z = h @ w_in[l]
        qa, ka, va, qb, kb, vb, qc, kc, vc, gz = jnp.split(z, split_points, axis=-1)
        qa = qa.reshape(bn, s_len, A_HEADS, 2, A_QK_DIM)
        ka = ka.reshape(bn, s_len, A_HEADS, 2, A_QK_DIM)
        va = va.reshape(bn, s_len, A_HEADS, A_V_DIM)
        qb = qb.reshape(bn, s_len, B_HEADS, HEAD_DIM)
        kb = kb.reshape(bn, s_len, B_KV_HEADS, HEAD_DIM)
        vb = vb.reshape(bn, s_len, B_KV_HEADS, HEAD_DIM)
        qc = qc.reshape(bn, s_len, C_HEADS, HEAD_DIM)
        kc = kc.reshape(bn, s_len, C_HEADS, HEAD_DIM)
        vc = vc.reshape(bn, s_len, C_HEADS, HEAD_DIM)
        gates = jax.nn.sigmoid(gz.astype(jnp.float32)).astype(x.dtype).reshape(bn, s_len, N_BRANCH, d)

        lam_init = 0.8 - 0.6 * math.exp(-0.3 * l)
        lp = diff_lambda[l].astype(jnp.float32)
        lam = jnp.exp(jnp.sum(lp[0] * lp[1])) - jnp.exp(jnp.sum(lp[2] * lp[3])) + lam_init

        ya = _diff_attention(qa, ka, va, lam, lam_init, diff_subln[l], t5_bias)
        yb = _gqa_axial(qb, kb, vb, qk_norm_b[l, 0], qk_norm_b[l, 1])
        yc = _neighbourhood_attention(qc, kc, vc, na_rpb[l])

        merged = (gates[:, :, 0] * (ya @ w_branch_a[l])
                  + gates[:, :, 1] * (yb @ w_branch_b[l])
                  + gates[:, :, 2] * (yc @ w_branch_c[l]))
        x = x + merged @ w_out[l]

        h = _rmsnorm(x, norm_ffn[l])
        if l % 2 == 0:
            j = l // 2
            x = x + _swiglu(h, ffn_gate[j], ffn_up[j], ffn_down[j])
        else:
            j = l // 2
            x = x + _moe_swiglu(h, moe_router[j], moe_gate[j], moe_up[j], moe_down[j])
    return _rmsnorm(x, norm_final)
```

```python
import functools
import math

import jax
import jax.numpy as jnp
from jax import lax
from jax.experimental import pallas as pl
from jax.experimental.pallas import tpu as pltpu

F32 = jnp.float32
BF16 = jnp.bfloat16

D_MODEL = 2048
GRID_W = 64
HEAD_DIM = 128
NORM_EPS = 1e-6
A_HEADS = 4
A_QK_DIM = 64
A_WIDTH = A_HEADS * HEAD_DIM
B_HEADS = 8
B_KV_HEADS = 2
B_GROUP = B_HEADS // B_KV_HEADS
B_WIDTH = B_HEADS * HEAD_DIM
C_HEADS = 4
C_WIDTH = C_HEADS * HEAD_DIM
NA_ROWS = 8
NA_COLS = 16
NA_ROFF = 2 * NA_ROWS - 1
NA_COFF = 2 * NA_COLS - 1
T5_BUCKETS = 32
T5_MAX_DIST = 128
ROPE_THETA = 10000.0
N_EXPERTS = 8
TOP_K = 2
QKV_WIDTH = 2 * A_WIDTH + A_WIDTH + B_WIDTH + 2 * B_KV_HEADS * HEAD_DIM + 3 * C_WIDTH
LOG2E = 1.4426950408889634
MASK_VALUE = -1e30

LANE = 128
V7X_VMEM_BYTES = 64 * 1024 * 1024
VMEM_CAP_BYTES = 56 * 1024 * 1024

SLAB_QA, SLAB_KA, SLAB_VA = 0, 4, 8
SLAB_QB, SLAB_KB, SLAB_VB = 12, 20, 22
SLAB_QC, SLAB_KC, SLAB_VC = 24, 28, 32
N_SLABS = QKV_WIDTH // LANE

TM_NORM = 512
TM_PROJ, TN_PROJ = 1024, 1536
TS_PREP = 1024
T_DIFF = 512
TQ_GQA, TK_GQA = 512, 512
NA_GROUP_ROWS = 8
NA_WIN_ROWS = 16
TM_MERGE, TN_MERGE = 512, 1024
TM_OUT, TN_OUT = 1024, 1024
TM_GLU, TF_GLU = 1024, 512
TM_DOWN, TN_DOWN = 512, 512
TM_ROUTE = 512
TM_MOE, TF_MOE = 512, 512
TM_COMB = 256


def _cparams(semantics, vmem_bytes):
    return pltpu.CompilerParams(dimension_semantics=semantics,
                                vmem_limit_bytes=int(min(vmem_bytes, VMEM_CAP_BYTES)))


def _nbytes(shape, dtype):
    return math.prod(shape) * jnp.dtype(dtype).itemsize


def _rmsnorm_rows(x, g):
    ms = jnp.mean(x * x, axis=-1, keepdims=True)
    return x * lax.rsqrt(ms + NORM_EPS) * g


def _rmsnorm_kernel(x_ref, g_ref, o_ref):
    o_ref[...] = _rmsnorm_rows(x_ref[...], g_ref[...]).astype(o_ref.dtype)


def _rmsnorm(x, g, out_dtype):
    s, d = x.shape
    tm = TM_NORM
    est = 2 * (_nbytes((tm, d), F32) + _nbytes((tm, d), out_dtype)) + 3 * _nbytes((tm, d), F32)
    return pl.pallas_call(
        _rmsnorm_kernel,
        grid=(s // tm,),
        in_specs=[pl.BlockSpec((tm, d), lambda i: (i, 0)),
                  pl.BlockSpec((1, d), lambda i: (0, 0))],
        out_specs=pl.BlockSpec((tm, d), lambda i: (i, 0)),
        out_shape=jax.ShapeDtypeStruct((s, d), out_dtype),
        compiler_params=_cparams(("parallel",), est),
        name="rmsnorm",
    )(x, g.reshape(1, d))


def _proj_slab_kernel(a_ref, w_ref, cs_ref, o_ref):
    acc = jnp.dot(a_ref[...], w_ref[...], preferred_element_type=F32) * cs_ref[...]
    for c in range(o_ref.shape[0]):
        o_ref[c] = acc[:, c * LANE:(c + 1) * LANE].astype(o_ref.dtype)


def _proj_slabs(a, w, colscale):
    s, k = a.shape
    n = w.shape[1]
    tm, tn = TM_PROJ, TN_PROJ
    spt = tn // LANE
    est = (2 * (_nbytes((tm, k), BF16) + _nbytes((k, tn), BF16) + _nbytes((tm, tn), BF16))
           + 2 * _nbytes((tm, tn), F32))
    return pl.pallas_call(
        _proj_slab_kernel,
        grid=(s // tm, n // tn),
        in_specs=[pl.BlockSpec((tm, k), lambda i, j: (i, 0)),
                  pl.BlockSpec((k, tn), lambda i, j: (0, j)),
                  pl.BlockSpec((1, tn), lambda i, j: (0, j))],
        out_specs=pl.BlockSpec((spt, tm, LANE), lambda i, j: (j, i, 0)),
        out_shape=jax.ShapeDtypeStruct((n // LANE, s, LANE), BF16),
        compiler_params=_cparams(("parallel", "arbitrary"), est),
        name="qkv_proj",
    )(a, w, colscale)


def _sigmoid(x):
    return 1.0 / (1.0 + jnp.exp(-x))


def _matmul_sigmoid_kernel(a_ref, w_ref, o_ref):
    acc = jnp.dot(a_ref[...], w_ref[...], preferred_element_type=F32)
    o_ref[...] = _sigmoid(acc).astype(o_ref.dtype)


def _matmul_sigmoid(a, w):
    s, k = a.shape
    n = w.shape[1]
    tm, tn = TM_PROJ, TN_PROJ
    est = (2 * (_nbytes((tm, k), BF16) + _nbytes((k, tn), BF16) + _nbytes((tm, tn), BF16))
           + 2 * _nbytes((tm, tn), F32))
    return pl.pallas_call(
        _matmul_sigmoid_kernel,
        grid=(s // tm, n // tn),
        in_specs=[pl.BlockSpec((tm, k), lambda i, j: (i, 0)),
                  pl.BlockSpec((k, tn), lambda i, j: (0, j))],
        out_specs=pl.BlockSpec((tm, tn), lambda i, j: (i, j)),
        out_shape=jax.ShapeDtypeStruct((s, n), BF16),
        compiler_params=_cparams(("parallel", "arbitrary"), est),
        name="gate_proj",
    )(a, w)


def _matmul_residual_kernel(a_ref, w_ref, r_ref, o_ref):
    o_ref[...] = r_ref[...] + jnp.dot(a_ref[...], w_ref[...], preferred_element_type=F32)


def _matmul_residual(a, w, r, tm, tn, name):
    s, k = a.shape
    n = w.shape[1]
    est = (2 * (_nbytes((tm, k), BF16) + _nbytes((k, tn), BF16) + 2 * _nbytes((tm, tn), F32))
           + _nbytes((tm, tn), F32))
    return pl.pallas_call(
        _matmul_residual_kernel,
        grid=(s // tm, n // tn),
        in_specs=[pl.BlockSpec((tm, k), lambda i, j: (i, 0)),
                  pl.BlockSpec((k, tn), lambda i, j: (0, j)),
                  pl.BlockSpec((tm, tn), lambda i, j: (i, j))],
        out_specs=pl.BlockSpec((tm, tn), lambda i, j: (i, j)),
        out_shape=jax.ShapeDtypeStruct((s, n), F32),
        compiler_params=_cparams(("parallel", "arbitrary"), est),
        name=name,
    )(a, w, r)


def _merge_kernel(ya_ref, yb_ref, yc_ref, wa_ref, wb_ref, wc_ref, ga_ref, gb_ref, gc_ref, o_ref):
    acc = ga_ref[...].astype(F32) * jnp.dot(ya_ref[...], wa_ref[...], preferred_element_type=F32)
    acc += gb_ref[...].astype(F32) * jnp.dot(yb_ref[...], wb_ref[...], preferred_element_type=F32)
    acc += gc_ref[...].astype(F32) * jnp.dot(yc_ref[...], wc_ref[...], preferred_element_type=F32)
    o_ref[...] = acc.astype(o_ref.dtype)


def _merge(ya, yb, yc, wa, wb, wc, gates):
    s = ya.shape[0]
    d = wa.shape[1]
    tm, tn = TM_MERGE, TN_MERGE
    nj = d // tn
    kin = A_WIDTH + B_WIDTH + C_WIDTH
    est = (2 * (_nbytes((tm, kin), BF16) + _nbytes((kin, tn), BF16) + 4 * _nbytes((tm, tn), BF16))
           + 4 * _nbytes((tm, tn), F32))
    return pl.pallas_call(
        _merge_kernel,
        grid=(s // tm, nj),
        in_specs=[pl.BlockSpec((tm, A_WIDTH), lambda i, j: (i, 0)),
                  pl.BlockSpec((tm, B_WIDTH), lambda i, j: (i, 0)),
                  pl.BlockSpec((tm, C_WIDTH), lambda i, j: (i, 0)),
                  pl.BlockSpec((A_WIDTH, tn), lambda i, j: (0, j)),
                  pl.BlockSpec((B_WIDTH, tn), lambda i, j: (0, j)),
                  pl.BlockSpec((C_WIDTH, tn), lambda i, j: (0, j)),
                  pl.BlockSpec((tm, tn), lambda i, j: (i, j)),
                  pl.BlockSpec((tm, tn), lambda i, j: (i, nj + j)),
                  pl.BlockSpec((tm, tn), lambda i, j: (i, 2 * nj + j))],
        out_specs=pl.BlockSpec((tm, tn), lambda i, j: (i, j)),
        out_shape=jax.ShapeDtypeStruct((s, d), BF16),
        compiler_params=_cparams(("parallel", "arbitrary"), est),
        name="branch_merge",
    )(ya, yb, yc, wa, wb, wc, gates, gates, gates)


def _glu_kernel(h_ref, wg_ref, wu_ref, o_ref):
    h = h_ref[...]
    g = jnp.dot(h, wg_ref[...], preferred_element_type=F32)
    u = jnp.dot(h, wu_ref[...], preferred_element_type=F32)
    o_ref[...] = (g * _sigmoid(g) * u).astype(o_ref.dtype)


def _glu(h, wg, wu):
    s, k = h.shape
    f = wg.shape[1]
    tm, tf = TM_GLU, TF_GLU
    est = (2 * (_nbytes((tm, k), BF16) + 2 * _nbytes((k, tf), BF16) + _nbytes((tm, tf), BF16))
           + 3 * _nbytes((tm, tf), F32))
    return pl.pallas_call(
        _glu_kernel,
        grid=(s // tm, f // tf),
        in_specs=[pl.BlockSpec((tm, k), lambda i, j: (i, 0)),
                  pl.BlockSpec((k, tf), lambda i, j: (0, j)),
                  pl.BlockSpec((k, tf), lambda i, j: (0, j))],
        out_specs=pl.BlockSpec((tm, tf), lambda i, j: (i, j)),
        out_shape=jax.ShapeDtypeStruct((s, f), BF16),
        compiler_params=_cparams(("parallel", "arbitrary"), est),
        name="ffn_glu",
    )(h, wg, wu)


def _rope_tables(s):
    half = HEAD_DIM // 2
    pos = jnp.arange(s, dtype=jnp.int32)
    row = (pos // GRID_W).astype(F32)
    col = (pos % GRID_W).astype(F32)
    inv = ROPE_THETA ** (-jnp.arange(0, half, 2, dtype=F32) / half)
    ang_r = row[:, None] * inv[None, :]
    ang_c = col[:, None] * inv[None, :]
    ang = jnp.concatenate([ang_r, ang_r, ang_c, ang_c], axis=-1)
    sign = jnp.tile(jnp.concatenate([-jnp.ones((half // 2,), F32), jnp.ones((half // 2,), F32)]), 2)
    return jnp.cos(ang), jnp.sin(ang) * sign[None, :]


def _qk_prep_kernel(z_ref, g_ref, cos_ref, sin_ref, o_ref):
    y = _rmsnorm_rows(z_ref[...].astype(F32), g_ref[...])
    lane = lax.broadcasted_iota(jnp.int32, y.shape, 1)
    quarter = HEAD_DIM // 4
    first = (lane % (2 * quarter)) < quarter
    partner = jnp.where(first, pltpu.roll(y, HEAD_DIM - quarter, 1), pltpu.roll(y, quarter, 1))
    o_ref[...] = (y * cos_ref[...] + partner * sin_ref[...]).astype(o_ref.dtype)


def _qk_prep(z3, gains, cos, sin_signed):
    s = z3.shape[1]
    ts = TS_PREP
    nsl = B_HEADS + B_KV_HEADS
    est = 2 * (2 * _nbytes((ts, LANE), BF16) + 2 * _nbytes((ts, LANE), F32)) + 6 * _nbytes((ts, LANE), F32)
    return pl.pallas_call(
        _qk_prep_kernel,
        grid=(s // ts, nsl),
        in_specs=[pl.BlockSpec((None, ts, LANE), lambda i, n: (SLAB_QB + n, i, 0)),
                  pl.BlockSpec((None, 1, LANE), lambda i, n: (n, 0, 0)),
                  pl.BlockSpec((ts, LANE), lambda i, n: (i, 0)),
                  pl.BlockSpec((ts, LANE), lambda i, n: (i, 0))],
        out_specs=pl.BlockSpec((None, ts, LANE), lambda i, n: (n, i, 0)),
        out_shape=jax.ShapeDtypeStruct((nsl, s, LANE), BF16),
        compiler_params=_cparams(("parallel", "arbitrary"), est),
        name="gqa_qk_prep",
    )(z3, gains, cos, sin_signed)


def _softmax_update(s, v, m_ref, l_ref, acc_ref):
    m_prev = m_ref[...]
    m_new = jnp.maximum(m_prev, jnp.max(s, axis=-1, keepdims=True))
    alpha = jnp.exp2(m_prev - m_new)
    p = jnp.exp2(s - m_new)
    l_ref[...] = alpha * l_ref[...] + jnp.sum(p, axis=-1, keepdims=True)
    acc_ref[...] = alpha * acc_ref[...] + jnp.dot(p.astype(BF16), v, preferred_element_type=F32)
    m_ref[...] = m_new


def _softmax_init(m_ref, l_ref, acc_ref):
    m_ref[...] = jnp.full(m_ref.shape, -jnp.inf, F32)
    l_ref[...] = jnp.zeros(l_ref.shape, F32)
    acc_ref[...] = jnp.zeros(acc_ref.shape, F32)


def _qk_scores(q, k):
    return lax.dot_general(q, k, (((1,), (1,)), ((), ())), preferred_element_type=F32)


def _gqa_kernel(q_ref, k_ref, v_ref, o_ref, m_ref, l_ref, acc_ref):
    ki = pl.program_id(2)
    tq = q_ref.shape[1]

    @pl.when(ki == 0)
    def _():
        _softmax_init(m_ref, l_ref, acc_ref)

    q = q_ref[...].reshape(B_GROUP * tq, HEAD_DIM)
    _softmax_update(_qk_scores(q, k_ref[...]), v_ref[...], m_ref, l_ref, acc_ref)

    @pl.when(ki == pl.num_programs(2) - 1)
    def _():
        o = acc_ref[...] / l_ref[...]
        for g in range(B_GROUP):
            o_ref[:, g * HEAD_DIM:(g + 1) * HEAD_DIM] = o[g * tq:(g + 1) * tq].astype(o_ref.dtype)


def _gqa(qk, z3):
    s = z3.shape[1]
    tq, tk = TQ_GQA, TK_GQA
    rows = B_GROUP * tq
    est = (2 * (_nbytes((rows, LANE), BF16) + 2 * _nbytes((tk, LANE), BF16) + _nbytes((rows, LANE), BF16))
           + 3 * _nbytes((rows, LANE), F32) + 3 * _nbytes((rows, tk), F32))
    return pl.pallas_call(
        _gqa_kernel,
        grid=(B_KV_HEADS, s // tq, s // tk),
        in_specs=[pl.BlockSpec((B_GROUP, tq, LANE), lambda h, i, j: (h, i, 0)),
                  pl.BlockSpec((None, tk, LANE), lambda h, i, j: (B_HEADS + h, j, 0)),
                  pl.BlockSpec((None, tk, LANE), lambda h, i, j: (SLAB_VB + h, j, 0))],
        out_specs=pl.BlockSpec((tq, B_GROUP * HEAD_DIM), lambda h, i, j: (i, h)),
        out_shape=jax.ShapeDtypeStruct((s, B_WIDTH), BF16),
        scratch_shapes=[pltpu.VMEM((rows, 1), F32), pltpu.VMEM((rows, 1), F32),
                        pltpu.VMEM((rows, HEAD_DIM), F32)],
        compiler_params=_cparams(("parallel", "parallel", "arbitrary"), est),
        name="gqa_attention",
    )(qk, qk, z3)


def _t5_bias_tile(t5_ref, h, block_offset, t):
    qpos = lax.broadcasted_iota(jnp.int32, (t, t), 0)
    kpos = lax.broadcasted_iota(jnp.int32, (t, t), 1) + block_offset * t
    rel = kpos - qpos
    nb = T5_BUCKETS // 2
    max_exact = nb // 2
    ret = jnp.where(rel > 0, nb, 0)
    n = jnp.abs(rel)
    nf = jnp.maximum(n, 1).astype(F32)
    large = max_exact + (jnp.log(nf / max_exact) / math.log(T5_MAX_DIST / max_exact)
                         * (nb - max_exact)).astype(jnp.int32)
    large = jnp.minimum(large, nb - 1)
    bucket = ret + jnp.where(n < max_exact, n, large)
    bias = jnp.full((t, t), t5_ref[h], F32)
    for b in range(1, T5_BUCKETS):
        bias = jnp.where(bucket == b, t5_ref[b * A_HEADS + h], bias)
    return bias * LOG2E


def _diff_kernel(t5_ref, lam_ref, q_ref, k_ref, v_ref, g_ref, o_ref,
                 qs_ref, bias_ref, m_ref, l_ref, acc_ref):
    h, qi, ki = pl.program_id(0), pl.program_id(1), pl.program_id(2)
    t = q_ref.shape[0]

    @pl.when((qi == 0) & (ki == 0))
    def _():
        for d in range(3):
            bias_ref[d] = _t5_bias_tile(t5_ref, h, d - 1, t)

    @pl.when(ki == 0)
    def _():
        _softmax_init(m_ref, l_ref, acc_ref)
        q = q_ref[...]
        lane = lax.broadcasted_iota(jnp.int32, q.shape, 1)
        zero = jnp.zeros_like(q)
        qs_ref[0:t, :] = jnp.where(lane < A_QK_DIM, q, zero)
        qs_ref[t:2 * t, :] = jnp.where(lane >= A_QK_DIM, q, zero)

    d = ki - qi

    def scores():
        return _qk_scores(qs_ref[...], k_ref[...])

    @pl.when(jnp.abs(d) <= 1)
    def _():
        s = (scores().reshape(2, t, t) + bias_ref[d + 1][None]).reshape(2 * t, t)
        _softmax_update(s, v_ref[...], m_ref, l_ref, acc_ref)

    @pl.when(d < -1)
    def _():
        c = t5_ref[(T5_BUCKETS // 2 - 1) * A_HEADS + h] * LOG2E
        _softmax_update(scores() + c, v_ref[...], m_ref, l_ref, acc_ref)

    @pl.when(d > 1)
    def _():
        c = t5_ref[(T5_BUCKETS - 1) * A_HEADS + h] * LOG2E
        _softmax_update(scores() + c, v_ref[...], m_ref, l_ref, acc_ref)

    @pl.when(ki == pl.num_programs(2) - 1)
    def _():
        o = acc_ref[...] / l_ref[...]
        o = o[0:t] - lam_ref[0] * o[t:2 * t]
        o_ref[...] = (_rmsnorm_rows(o, g_ref[...]) * lam_ref[1]).astype(o_ref.dtype)


def _diff_attention(z3, t5_flat, lam_pair, subln_g):
    s = z3.shape[1]
    t = T_DIFF
    assert t >= T5_MAX_DIST and s % t == 0
    est = (2 * (3 * _nbytes((t, LANE), BF16) + _nbytes((t, LANE), BF16))
           + _nbytes((2 * t, LANE), BF16) + 3 * _nbytes((t, t), F32)
           + 3 * _nbytes((2 * t, LANE), F32) + 3 * _nbytes((2 * t, t), F32))
    grid_spec = pltpu.PrefetchScalarGridSpec(
        num_scalar_prefetch=2,
        grid=(A_HEADS, s // t, s // t),
        in_specs=[pl.BlockSpec((None, t, LANE), lambda h, i, j, *_: (SLAB_QA + h, i, 0)),
                  pl.BlockSpec((None, t, LANE), lambda h, i, j, *_: (SLAB_KA + h, j, 0)),
                  pl.BlockSpec((None, t, LANE), lambda h, i, j, *_: (SLAB_VA + h, j, 0)),
                  pl.BlockSpec((1, LANE), lambda h, i, j, *_: (0, 0))],
        out_specs=pl.BlockSpec((t, HEAD_DIM), lambda h, i, j, *_: (i, h)),
        scratch_shapes=[pltpu.VMEM((2 * t, LANE), BF16), pltpu.VMEM((3, t, t), F32),
                        pltpu.VMEM((2 * t, 1), F32), pltpu.VMEM((2 * t, 1), F32),
                        pltpu.VMEM((2 * t, HEAD_DIM), F32)])
    return pl.pallas_call(
        _diff_kernel,
        grid_spec=grid_spec,
        out_shape=jax.ShapeDtypeStruct((s, A_WIDTH), BF16),
        compiler_params=_cparams(("arbitrary", "arbitrary", "arbitrary"), est),
        name="diff_attention",
    )(t5_flat, lam_pair, z3, z3, z3, subln_g.reshape(1, LANE))


def _na_window_start(first_row, rows):
    return max(0, min(first_row - NA_ROWS // 2, rows - NA_WIN_ROWS))


def _na_build_bias(rpb_ref, h, bias_ref, rows):
    cq = lax.broadcasted_iota(jnp.int32, (GRID_W, LANE), 0)
    ck = lax.broadcasted_iota(jnp.int32, (GRID_W, LANE), 1) % GRID_W
    cs = jnp.clip(cq - NA_COLS // 2, 0, GRID_W - NA_COLS)
    col_valid = (ck >= cs) & (ck < cs + NA_COLS)
    coff = jnp.clip(ck - cq + (NA_COLS - 1), 0, NA_COFF - 1)
    lower_half = lax.broadcasted_iota(jnp.int32, (GRID_W, LANE), 1) < GRID_W
    masked = jnp.full((GRID_W, LANE), MASK_VALUE, F32)
    col_bias = []
    for ro in range(NA_ROFF):
        base = (h * NA_ROFF + ro) * NA_COFF
        b = jnp.full((GRID_W, LANE), rpb_ref[base], F32)
        for c in range(1, NA_COFF):
            b = jnp.where(coff == c, rpb_ref[base + c], b)
        col_bias.append(jnp.where(col_valid, b * LOG2E, MASK_VALUE))
    n_groups = rows // NA_GROUP_ROWS
    for var, group in enumerate((0, 1, n_groups - 1)):
        ws = _na_window_start(group * NA_GROUP_ROWS, rows)
        for a in range(NA_GROUP_ROWS):
            r = group * NA_GROUP_ROWS + a
            rs = max(0, min(r - NA_ROWS // 2, rows - NA_ROWS))
            for ip in range(NA_WIN_ROWS // 2):
                halves = []
                for kr in (ws + 2 * ip, ws + 2 * ip + 1):
                    valid = rs <= kr < rs + NA_ROWS
                    halves.append(col_bias[kr - r + NA_ROWS - 1] if valid else masked)
                bias_ref[var, a * GRID_W:(a + 1) * GRID_W, ip * LANE:(ip + 1) * LANE] = (
                    jnp.where(lower_half, halves[0], halves[1]))


def _na_kernel(rpb_ref, q_ref, k_ref, v_ref, o_ref, bias_ref, *, rows):
    h, g = pl.program_id(0), pl.program_id(1)
    n_groups = rows // NA_GROUP_ROWS
    win = NA_WIN_ROWS * GRID_W

    @pl.when(g == 0)
    def _():
        _na_build_bias(rpb_ref, h, bias_ref, rows)

    ws = jnp.clip(g * NA_GROUP_ROWS - NA_ROWS // 2, 0, rows - NA_WIN_ROWS)
    start = pl.multiple_of(ws * GRID_W, (NA_ROWS // 2) * GRID_W)
    kw = k_ref[pl.ds(start, win), :]
    vw = v_ref[pl.ds(start, win), :]
    var = jnp.where(g == 0, 0, jnp.where(g == n_groups - 1, 2, 1))
    s = _qk_scores(q_ref[...], kw) + bias_ref[var]
    m = jnp.max(s, axis=-1, keepdims=True)
    p = jnp.exp2(s - m)
    l = jnp.sum(p, axis=-1, keepdims=True)
    o = jnp.dot(p.astype(BF16), vw, preferred_element_type=F32) / l
    o_ref[...] = o.astype(o_ref.dtype)


def _neighbourhood_attention(z3, rpb_flat):
    s = z3.shape[1]
    rows = s // GRID_W
    assert rows % NA_GROUP_ROWS == 0 and rows >= 3 * NA_GROUP_ROWS
    tq = NA_GROUP_ROWS * GRID_W
    win = NA_WIN_ROWS * GRID_W
    est = (2 * (2 * _nbytes((tq, LANE), BF16) + 2 * _nbytes((s, LANE), BF16))
           + 3 * _nbytes((tq, win), F32) + 3 * _nbytes((tq, win), F32))
    grid_spec = pltpu.PrefetchScalarGridSpec(
        num_scalar_prefetch=1,
        grid=(C_HEADS, rows // NA_GROUP_ROWS),
        in_specs=[pl.BlockSpec((None, tq, LANE), lambda h, g, *_: (SLAB_QC + h, g, 0)),
                  pl.BlockSpec((None, s, LANE), lambda h, g, *_: (SLAB_KC + h, 0, 0)),
                  pl.BlockSpec((None, s, LANE), lambda h, g, *_: (SLAB_VC + h, 0, 0))],
        out_specs=pl.BlockSpec((tq, HEAD_DIM), lambda h, g, *_: (g, h)),
        scratch_shapes=[pltpu.VMEM((3, tq, win), F32)])
    return pl.pallas_call(
        functools.partial(_na_kernel, rows=rows),
        grid_spec=grid_spec,
        out_shape=jax.ShapeDtypeStruct((s, C_WIDTH), BF16),
        compiler_params=_cparams(("arbitrary", "arbitrary"), est),
        name="neighbourhood_attention",
    )(rpb_flat, z3, z3, z3)


def _router_kernel(x_ref, g_ref, wr_ref, o_ref):
    h = _rmsnorm_rows(x_ref[...], g_ref[...])
    logits = jnp.dot(h, wr_ref[...], preferred_element_type=F32, precision=lax.Precision.HIGHEST)
    lane = lax.broadcasted_iota(jnp.int32, logits.shape, 1)
    lane_f = lane.astype(F32)
    logits = jnp.where(lane < N_EXPERTS, logits, -jnp.inf)
    m1 = jnp.max(logits, axis=-1, keepdims=True)
    i1 = jnp.min(jnp.where(logits == m1, lane_f, float(LANE)), axis=-1, keepdims=True)
    rest = jnp.where(lane_f == i1, -jnp.inf, logits)
    m2 = jnp.max(rest, axis=-1, keepdims=True)
    i2 = jnp.min(jnp.where(rest == m2, lane_f, float(LANE)), axis=-1, keepdims=True)
    e = jnp.exp(m2 - m1)
    g1 = 1.0 / (1.0 + e)
    g2 = e / (1.0 + e)
    o_ref[...] = jnp.where(lane == 0, i1, jnp.where(lane == 1, i2, jnp.where(lane == 2, g1, g2)))


def _router(x, g, w_router):
    s, d = x.shape
    tm = TM_ROUTE
    wr = jnp.zeros((d, LANE), F32).at[:, :N_EXPERTS].set(w_router)
    est = 2 * (_nbytes((tm, d), F32) + _nbytes((d, LANE), F32) + _nbytes((tm, LANE), F32)) + 4 * _nbytes((tm, d), F32)
    return pl.pallas_call(
        _router_kernel,
        grid=(s // tm,),
        in_specs=[pl.BlockSpec((tm, d), lambda i: (i, 0)),
                  pl.BlockSpec((1, d), lambda i: (0, 0)),
                  pl.BlockSpec((d, LANE), lambda i: (0, 0))],
        out_specs=pl.BlockSpec((tm, LANE), lambda i: (i, 0)),
        out_shape=jax.ShapeDtypeStruct((s, LANE), F32),
        compiler_params=_cparams(("parallel",), est),
        name="moe_router",
    )(x, g.reshape(1, d), wr)


def _row_gather(src_hbm, dst_ref, sem, n_rows, row_of):
    def issue(r, carry):
        pltpu.make_async_copy(src_hbm.at[pl.ds(row_of(r), 1)], dst_ref.at[pl.ds(r, 1)], sem).start()
        return carry
    lax.fori_loop(0, n_rows, issue, 0)
    pltpu.make_async_copy(src_hbm.at[pl.ds(0, n_rows)], dst_ref.at[pl.ds(0, n_rows)], sem).wait()


def _moe_kernel(tile_e_ref, nused_ref, tok_ref, x_hbm, gn_ref, wg_ref, wu_ref, wd_ref, bg_ref,
                o_ref, xg_ref, xb_ref, acc_ref, sem):
    i, f = pl.program_id(0), pl.program_id(1)
    tm = xg_ref.shape[0]
    used = i < nused_ref[0]
    last = f == pl.num_programs(1) - 1

    @pl.when(used & (f == 0))
    def _():
        _row_gather(x_hbm, xg_ref, sem, tm, lambda r: tok_ref[i * tm + r])
        xb_ref[...] = _rmsnorm_rows(xg_ref[...], gn_ref[...]).astype(BF16)
        acc_ref[...] = jnp.zeros(acc_ref.shape, F32)

    @pl.when(used)
    def _():
        hb = xb_ref[...]
        g = jnp.dot(hb, wg_ref[...], preferred_element_type=F32)
        u = jnp.dot(hb, wu_ref[...], preferred_element_type=F32)
        act = (g * _sigmoid(g) * u).astype(BF16)
        acc_ref[...] += jnp.dot(act, wd_ref[...], preferred_element_type=F32)

    @pl.when(used & last)
    def _():
        o_ref[...] = acc_ref[...] * bg_ref[...]

    @pl.when(jnp.logical_not(used) & last)
    def _():
        o_ref[...] = jnp.zeros(o_ref.shape, F32)


def _moe_experts(x, gn, wg, wu, wd, tile_e, n_used, buf_tok, buf_g):
    d = x.shape[1]
    fdim = wg.shape[2]
    cap = buf_tok.shape[0]
    tm, tf = TM_MOE, TF_MOE
    nf = fdim // tf

    def fsel(i, f, nused):
        return jnp.where(i < nused[0], f, nf - 1)

    est = (2 * (3 * _nbytes((d, tf), BF16) + _nbytes((tm, d), F32) + _nbytes((tm, LANE), F32))
           + 2 * _nbytes((tm, d), F32) + _nbytes((tm, d), BF16)
           + 3 * _nbytes((tm, tf), F32) + _nbytes((tm, d), F32))
    grid_spec = pltpu.PrefetchScalarGridSpec(
        num_scalar_prefetch=3,
        grid=(cap // tm, nf),
        in_specs=[pl.BlockSpec(memory_space=pl.ANY),
                  pl.BlockSpec((1, d), lambda i, f, te, nu, tok: (0, 0)),
                  pl.BlockSpec((None, d, tf), lambda i, f, te, nu, tok: (te[i], 0, fsel(i, f, nu))),
                  pl.BlockSpec((None, d, tf), lambda i, f, te, nu, tok: (te[i], 0, fsel(i, f, nu))),
                  pl.BlockSpec((None, tf, d), lambda i, f, te, nu, tok: (te[i], fsel(i, f, nu), 0)),
                  pl.BlockSpec((tm, 1), lambda i, f, te, nu, tok: (i, 0))],
        out_specs=pl.BlockSpec((tm, d), lambda i, f, te, nu, tok: (i, 0)),
        scratch_shapes=[pltpu.VMEM((tm, d), F32), pltpu.VMEM((tm, d), BF16),
                        pltpu.VMEM((tm, d), F32), pltpu.SemaphoreType.DMA(())])
    return pl.pallas_call(
        _moe_kernel,
        grid_spec=grid_spec,
        out_shape=jax.ShapeDtypeStruct((cap, d), F32),
        compiler_params=_cparams(("arbitrary", "arbitrary"), est),
        name="moe_experts",
    )(tile_e, n_used, buf_tok, x, gn.reshape(1, d), wg, wu, wd, buf_g.reshape(cap, 1))


def _combine_kernel(pos_ref, x_ref, y_hbm, o_ref, buf_ref, sem):
    i = pl.program_id(0)
    tm = x_ref.shape[0]
    _row_gather(y_hbm, buf_ref, sem, TOP_K * tm,
                lambda r: pos_ref[TOP_K * (i * tm + r % tm) + r // tm])
    o_ref[...] = x_ref[...] + (buf_ref[0:tm, :] + buf_ref[tm:2 * tm, :])


def _moe_combine(x, yb, pos_flat):
    s, d = x.shape
    tm = TM_COMB
    est = 4 * _nbytes((tm, d), F32) + _nbytes((TOP_K * tm, d), F32) + 2 * _nbytes((tm, d), F32)
    grid_spec = pltpu.PrefetchScalarGridSpec(
        num_scalar_prefetch=1,
        grid=(s // tm,),
        in_specs=[pl.BlockSpec((tm, d), lambda i, pos: (i, 0)),
                  pl.BlockSpec(memory_space=pl.ANY)],
        out_specs=pl.BlockSpec((tm, d), lambda i, pos: (i, 0)),
        scratch_shapes=[pltpu.VMEM((TOP_K * tm, d), F32), pltpu.SemaphoreType.DMA(())])
    return pl.pallas_call(
        _combine_kernel,
        grid_spec=grid_spec,
        out_shape=jax.ShapeDtypeStruct((s, d), F32),
        compiler_params=_cparams(("arbitrary",), est),
        name="moe_combine",
    )(pos_flat, x, yb)


def _moe_layer(x, gn, w_router, wg, wu, wd):
    s = x.shape[0]
    tm = TM_MOE
    route = _router(x, gn, w_router)
    e_flat = route[:, 0:TOP_K].astype(jnp.int32).reshape(-1)
    g_flat = route[:, TOP_K:2 * TOP_K].reshape(-1)
    n_assign = s * TOP_K
    onehot = (e_flat[:, None] == jnp.arange(N_EXPERTS, dtype=jnp.int32)[None, :]).astype(jnp.int32)
    csum = jnp.cumsum(onehot, axis=0)
    rank = jnp.sum(csum * onehot, axis=1) - 1
    counts = csum[-1]
    pcounts = (counts + tm - 1) // tm * tm
    pends = jnp.cumsum(pcounts)
    pstarts = pends - pcounts
    dest = jnp.sum(pstarts[None, :] * onehot, axis=1) + rank
    cap = n_assign + N_EXPERTS * tm
    tok_flat = jnp.arange(n_assign, dtype=jnp.int32) // TOP_K
    buf_tok = jnp.zeros((cap,), jnp.int32).at[dest].set(tok_flat)
    buf_g = jnp.zeros((cap,), F32).at[dest].set(g_flat)
    n_tiles = cap // tm
    tile_e = jnp.minimum(jnp.searchsorted(pends, jnp.arange(n_tiles, dtype=jnp.int32) * tm, side='right'),
                         N_EXPERTS - 1).astype(jnp.int32)
    n_used = (pends[-1:] // tm).astype(jnp.int32)
    yb = _moe_experts(x, gn, wg, wu, wd, tile_e, n_used, buf_tok, buf_g)
    return _moe_combine(x, yb, dest.astype(jnp.int32))


def kernel(x, w_in, w_branch_a, w_branch_b, w_branch_c, w_out, norm_mix, norm_ffn, norm_final, t5_bias, diff_lambda, diff_subln, qk_norm_b, na_rpb, ffn_gate, ffn_up, ffn_down, moe_router, moe_gate, moe_up, moe_down):
    bn, s, d = x.shape
    assert bn == 1 and d == D_MODEL
    depth = w_in.shape[0]
    xs = x.reshape(s, d)
    cos, sin_signed = _rope_tables(s)
    t5_flat = t5_bias.astype(F32).reshape(-1)
    colscale = jnp.ones((1, QKV_WIDTH), F32)
    colscale = colscale.at[:, SLAB_QA * LANE:SLAB_KA * LANE].set(A_QK_DIM ** -0.5 * LOG2E)
    colscale = colscale.at[:, SLAB_QC * LANE:SLAB_KC * LANE].set(HEAD_DIM ** -0.5 * LOG2E)

    for l in range(depth):
        h = _rmsnorm(xs, norm_mix[l], BF16)
        w_l = w_in[l].astype(BF16)
        z3 = _proj_slabs(h, w_l[:, :QKV_WIDTH], colscale)
        gates = _matmul_sigmoid(h, w_l[:, QKV_WIDTH:])

        lam_init = 0.8 - 0.6 * math.exp(-0.3 * l)
        lp = diff_lambda[l].astype(F32)
        lam = jnp.exp(jnp.sum(lp[0] * lp[1])) - jnp.exp(jnp.sum(lp[2] * lp[3])) + lam_init
        lam_pair = jnp.stack([lam, jnp.asarray(1.0 - lam_init, F32)]).astype(F32)
        ya = _diff_attention(z3, t5_flat, lam_pair, diff_subln[l].astype(F32))

        q_gain = qk_norm_b[l, 0].astype(F32) * (HEAD_DIM ** -0.5 * LOG2E)
        k_gain = qk_norm_b[l, 1].astype(F32)
        gains = jnp.concatenate([jnp.tile(q_gain[None], (B_HEADS, 1)),
                                 jnp.tile(k_gain[None], (B_KV_HEADS, 1))])[:, None, :]
        yb = _gqa(_qk_prep(z3, gains, cos, sin_signed), z3)

        yc = _neighbourhood_attention(z3, na_rpb[l].astype(F32).reshape(-1))

        merged = _merge(ya, yb, yc, w_branch_a[l].astype(BF16), w_branch_b[l].astype(BF16),
                        w_branch_c[l].astype(BF16), gates)
        xs = _matmul_residual(merged, w_out[l].astype(BF16), xs, TM_OUT, TN_OUT, "out_proj")

        j = l // 2
        if l % 2 == 0:
            hf = _rmsnorm(xs, norm_ffn[l], BF16)
            act = _glu(hf, ffn_gate[j].astype(BF16), ffn_up[j].astype(BF16))
            xs = _matmul_residual(act, ffn_down[j].astype(BF16), xs, TM_DOWN, TN_DOWN, "ffn_down")
        else:
            xs = _moe_layer(xs, norm_ffn[l], moe_router[j], moe_gate[j].astype(BF16),
                            moe_up[j].astype(BF16), moe_down[j].astype(BF16))
    return _rmsnorm(xs, norm_final, x.dtype).reshape(bn, s, d)
```

```python
import functools
import math

import jax
import jax.numpy as jnp
from jax import lax
from jax.experimental import pallas as pl
from jax.experimental.pallas import tpu as pltpu

F32 = jnp.float32
BF16 = jnp.bfloat16

D_MODEL = 2048
GRID_W = 64
HEAD_DIM = 128
NORM_EPS = 1e-6
A_HEADS = 4
A_QK_DIM = 64
A_WIDTH = A_HEADS * HEAD_DIM
B_HEADS = 8
B_KV_HEADS = 2
B_GROUP = B_HEADS // B_KV_HEADS
B_WIDTH = B_HEADS * HEAD_DIM
C_HEADS = 4
C_WIDTH = C_HEADS * HEAD_DIM
NA_ROWS = 8
NA_COLS = 16
NA_ROFF = 2 * NA_ROWS - 1
NA_COFF = 2 * NA_COLS - 1
T5_BUCKETS = 32
T5_MAX_DIST = 128
ROPE_THETA = 10000.0
N_EXPERTS = 8
TOP_K = 2
QKV_WIDTH = 2 * A_WIDTH + A_WIDTH + B_WIDTH + 2 * B_KV_HEADS * HEAD_DIM + 3 * C_WIDTH
LOG2E = 1.4426950408889634
MASK_VALUE = -1e30

LANE = 128
V7X_VMEM_BYTES = 64 * 1024 * 1024
VMEM_CAP_BYTES = 56 * 1024 * 1024
QCHUNK = 512

SLAB_QA, SLAB_KA, SLAB_VA = 0, 4, 8
SLAB_QB, SLAB_KB, SLAB_VB = 12, 20, 22
SLAB_QC, SLAB_KC, SLAB_VC = 24, 28, 32
N_SLABS = QKV_WIDTH // LANE
VT_A, VT_B = 0, A_HEADS
VT_ROWS = HEAD_DIM + 16

TM_NORM = 512
TM_PROJ, TN_PROJ = 1024, 1536
TS_PREP = 1024
TQ_DIFF, TK_DIFF = 1024, 512
TQ_GQA, TK_GQA = 512, 1024
NA_GROUP_ROWS = 8
NA_WIN_ROWS = 16
TM_MERGE, TN_MERGE = 512, 1024
TM_OUT, TN_OUT = 1024, 1024
TM_GLU, TF_GLU = 1024, 512
TM_DOWN, TN_DOWN = 512, 512
TM_ROUTE = 512
TM_MOE, TF_MOE = 512, 512
TM_COMB = 256


def _cparams(semantics, vmem_bytes):
    return pltpu.CompilerParams(dimension_semantics=semantics,
                                vmem_limit_bytes=int(min(vmem_bytes, VMEM_CAP_BYTES)))


def _nbytes(shape, dtype):
    return math.prod(shape) * jnp.dtype(dtype).itemsize


def _rmsnorm_rows(x, g):
    ms = jnp.mean(x * x, axis=-1, keepdims=True)
    return x * lax.rsqrt(ms + NORM_EPS) * g


def _rmsnorm_kernel(x_ref, g_ref, o_ref):
    o_ref[...] = _rmsnorm_rows(x_ref[...], g_ref[...]).astype(o_ref.dtype)


def _rmsnorm(x, g, out_dtype):
    s, d = x.shape
    tm = TM_NORM
    est = 2 * (_nbytes((tm, d), F32) + _nbytes((tm, d), out_dtype)) + 3 * _nbytes((tm, d), F32)
    return pl.pallas_call(
        _rmsnorm_kernel,
        grid=(s // tm,),
        in_specs=[pl.BlockSpec((tm, d), lambda i: (i, 0)),
                  pl.BlockSpec((1, d), lambda i: (0, 0))],
        out_specs=pl.BlockSpec((tm, d), lambda i: (i, 0)),
        out_shape=jax.ShapeDtypeStruct((s, d), out_dtype),
        compiler_params=_cparams(("parallel",), est),
        name="rmsnorm",
    )(x, g.reshape(1, d))


def _proj_slab_kernel(a_ref, w_ref, cs_ref, o_ref):
    acc = jnp.dot(a_ref[...], w_ref[...], preferred_element_type=F32) * cs_ref[...]
    for c in range(o_ref.shape[0]):
        o_ref[c] = acc[:, c * LANE:(c + 1) * LANE].astype(o_ref.dtype)


def _proj_slabs(a, w, colscale):
    s, k = a.shape
    n = w.shape[1]
    tm, tn = TM_PROJ, TN_PROJ
    spt = tn // LANE
    est = (2 * (_nbytes((tm, k), BF16) + _nbytes((k, tn), BF16) + _nbytes((tm, tn), BF16))
           + 2 * _nbytes((tm, tn), F32))
    return pl.pallas_call(
        _proj_slab_kernel,
        grid=(s // tm, n // tn),
        in_specs=[pl.BlockSpec((tm, k), lambda i, j: (i, 0)),
                  pl.BlockSpec((k, tn), lambda i, j: (0, j)),
                  pl.BlockSpec((1, tn), lambda i, j: (0, j))],
        out_specs=pl.BlockSpec((spt, tm, LANE), lambda i, j: (j, i, 0)),
        out_shape=jax.ShapeDtypeStruct((n // LANE, s, LANE), BF16),
        compiler_params=_cparams(("parallel", "arbitrary"), est),
        name="qkv_proj",
    )(a, w, colscale)


def _sigmoid(x):
    return 1.0 / (1.0 + jnp.exp(-x))


def _matmul_sigmoid_kernel(a_ref, w_ref, o_ref):
    acc = jnp.dot(a_ref[...], w_ref[...], preferred_element_type=F32)
    o_ref[...] = _sigmoid(acc).astype(o_ref.dtype)


def _matmul_sigmoid(a, w):
    s, k = a.shape
    n = w.shape[1]
    tm, tn = TM_PROJ, TN_PROJ
    est = (2 * (_nbytes((tm, k), BF16) + _nbytes((k, tn), BF16) + _nbytes((tm, tn), BF16))
           + 2 * _nbytes((tm, tn), F32))
    return pl.pallas_call(
        _matmul_sigmoid_kernel,
        grid=(s // tm, n // tn),
        in_specs=[pl.BlockSpec((tm, k), lambda i, j: (i, 0)),
                  pl.BlockSpec((k, tn), lambda i, j: (0, j))],
        out_specs=pl.BlockSpec((tm, tn), lambda i, j: (i, j)),
        out_shape=jax.ShapeDtypeStruct((s, n), BF16),
        compiler_params=_cparams(("parallel", "arbitrary"), est),
        name="gate_proj",
    )(a, w)


def _matmul_residual_kernel(a_ref, w_ref, r_ref, o_ref):
    o_ref[...] = r_ref[...] + jnp.dot(a_ref[...], w_ref[...], preferred_element_type=F32)


def _matmul_residual(a, w, r, tm, tn, name):
    s, k = a.shape
    n = w.shape[1]
    est = (2 * (_nbytes((tm, k), BF16) + _nbytes((k, tn), BF16) + 2 * _nbytes((tm, tn), F32))
           + _nbytes((tm, tn), F32))
    return pl.pallas_call(
        _matmul_residual_kernel,
        grid=(s // tm, n // tn),
        in_specs=[pl.BlockSpec((tm, k), lambda i, j: (i, 0)),
                  pl.BlockSpec((k, tn), lambda i, j: (0, j)),
                  pl.BlockSpec((tm, tn), lambda i, j: (i, j))],
        out_specs=pl.BlockSpec((tm, tn), lambda i, j: (i, j)),
        out_shape=jax.ShapeDtypeStruct((s, n), F32),
        compiler_params=_cparams(("parallel", "arbitrary"), est),
        name=name,
    )(a, w, r)


def _merge_kernel(ya_ref, yb_ref, yc_ref, wa_ref, wb_ref, wc_ref, ga_ref, gb_ref, gc_ref, o_ref):
    acc = ga_ref[...].astype(F32) * jnp.dot(ya_ref[...], wa_ref[...], preferred_element_type=F32)
    acc += gb_ref[...].astype(F32) * jnp.dot(yb_ref[...], wb_ref[...], preferred_element_type=F32)
    acc += gc_ref[...].astype(F32) * jnp.dot(yc_ref[...], wc_ref[...], preferred_element_type=F32)
    o_ref[...] = acc.astype(o_ref.dtype)


def _merge(ya, yb, yc, wa, wb, wc, gates):
    s = ya.shape[0]
    d = wa.shape[1]
    tm, tn = TM_MERGE, TN_MERGE
    nj = d // tn
    kin = A_WIDTH + B_WIDTH + C_WIDTH
    est = (2 * (_nbytes((tm, kin), BF16) + _nbytes((kin, tn), BF16) + 4 * _nbytes((tm, tn), BF16))
           + 4 * _nbytes((tm, tn), F32))
    return pl.pallas_call(
        _merge_kernel,
        grid=(s // tm, nj),
        in_specs=[pl.BlockSpec((tm, A_WIDTH), lambda i, j: (i, 0)),
                  pl.BlockSpec((tm, B_WIDTH), lambda i, j: (i, 0)),
                  pl.BlockSpec((tm, C_WIDTH), lambda i, j: (i, 0)),
                  pl.BlockSpec((A_WIDTH, tn), lambda i, j: (0, j)),
                  pl.BlockSpec((B_WIDTH, tn), lambda i, j: (0, j)),
                  pl.BlockSpec((C_WIDTH, tn), lambda i, j: (0, j)),
                  pl.BlockSpec((tm, tn), lambda i, j: (i, j)),
                  pl.BlockSpec((tm, tn), lambda i, j: (i, nj + j)),
                  pl.BlockSpec((tm, tn), lambda i, j: (i, 2 * nj + j))],
        out_specs=pl.BlockSpec((tm, tn), lambda i, j: (i, j)),
        out_shape=jax.ShapeDtypeStruct((s, d), BF16),
        compiler_params=_cparams(("parallel", "arbitrary"), est),
        name="branch_merge",
    )(ya, yb, yc, wa, wb, wc, gates, gates, gates)


def _glu_kernel(h_ref, wg_ref, wu_ref, o_ref):
    h = h_ref[...]
    g = jnp.dot(h, wg_ref[...], preferred_element_type=F32)
    u = jnp.dot(h, wu_ref[...], preferred_element_type=F32)
    o_ref[...] = (g * _sigmoid(g) * u).astype(o_ref.dtype)


def _glu(h, wg, wu):
    s, k = h.shape
    f = wg.shape[1]
    tm, tf = TM_GLU, TF_GLU
    est = (2 * (_nbytes((tm, k), BF16) + 2 * _nbytes((k, tf), BF16) + _nbytes((tm, tf), BF16))
           + 3 * _nbytes((tm, tf), F32))
    return pl.pallas_call(
        _glu_kernel,
        grid=(s // tm, f // tf),
        in_specs=[pl.BlockSpec((tm, k), lambda i, j: (i, 0)),
                  pl.BlockSpec((k, tf), lambda i, j: (0, j)),
                  pl.BlockSpec((k, tf), lambda i, j: (0, j))],
        out_specs=pl.BlockSpec((tm, tf), lambda i, j: (i, j)),
        out_shape=jax.ShapeDtypeStruct((s, f), BF16),
        compiler_params=_cparams(("parallel", "arbitrary"), est),
        name="ffn_glu",
    )(h, wg, wu)


def _rope_tables(s):
    half = HEAD_DIM // 2
    pos = jnp.arange(s, dtype=jnp.int32)
    row = (pos // GRID_W).astype(F32)
    col = (pos % GRID_W).astype(F32)
    inv = ROPE_THETA ** (-jnp.arange(0, half, 2, dtype=F32) / half)
    ang_r = row[:, None] * inv[None, :]
    ang_c = col[:, None] * inv[None, :]
    ang = jnp.concatenate([ang_r, ang_r, ang_c, ang_c], axis=-1)
    sign = jnp.tile(jnp.concatenate([-jnp.ones((half // 2,), F32), jnp.ones((half // 2,), F32)]), 2)
    return jnp.cos(ang), jnp.sin(ang) * sign[None, :]


def _qk_prep_kernel(z_ref, g_ref, cos_ref, sin_ref, o_ref):
    y = _rmsnorm_rows(z_ref[...].astype(F32), g_ref[...])
    lane = lax.broadcasted_iota(jnp.int32, y.shape, 1)
    quarter = HEAD_DIM // 4
    first = (lane % (2 * quarter)) < quarter
    partner = jnp.where(first, pltpu.roll(y, HEAD_DIM - quarter, 1), pltpu.roll(y, quarter, 1))
    o_ref[...] = (y * cos_ref[...] + partner * sin_ref[...]).astype(o_ref.dtype)


def _qk_prep(z3, gains, cos, sin_signed):
    s = z3.shape[1]
    ts = TS_PREP
    nsl = B_HEADS + B_KV_HEADS
    est = 2 * (2 * _nbytes((ts, LANE), BF16) + 2 * _nbytes((ts, LANE), F32)) + 6 * _nbytes((ts, LANE), F32)
    return pl.pallas_call(
        _qk_prep_kernel,
        grid=(s // ts, nsl),
        in_specs=[pl.BlockSpec((None, ts, LANE), lambda i, n: (SLAB_QB + n, i, 0)),
                  pl.BlockSpec((None, 1, LANE), lambda i, n: (n, 0, 0)),
                  pl.BlockSpec((ts, LANE), lambda i, n: (i, 0)),
                  pl.BlockSpec((ts, LANE), lambda i, n: (i, 0))],
        out_specs=pl.BlockSpec((None, ts, LANE), lambda i, n: (n, i, 0)),
        out_shape=jax.ShapeDtypeStruct((nsl, s, LANE), BF16),
        compiler_params=_cparams(("parallel", "arbitrary"), est),
        name="gqa_qk_prep",
    )(z3, gains, cos, sin_signed)


def _softmax_update_t(k, q_chunk, vt_aug, m_ref, acc_ref, bias_chunk=None, shift=None):
    n_chunks = m_ref.shape[1] // QCHUNK

    def scores(j):
        st = _qk_scores(k, q_chunk(j))
        return st if bias_chunk is None else st + bias_chunk(j)

    st_next = scores(0)
    for j in range(n_chunks):
        cols = slice(j * QCHUNK, (j + 1) * QCHUNK)
        st = st_next
        if j + 1 < n_chunks:
            st_next = scores(j + 1)
        m_prev = m_ref[:, cols]
        tile_max = jnp.max(st, axis=0, keepdims=True)
        if shift is None:
            m_new = jnp.maximum(m_prev, tile_max)
            sub = m_new
        else:
            m_new = jnp.maximum(m_prev, tile_max + shift)
            sub = m_new - shift
        alpha = jnp.exp2(m_prev - m_new)
        p = jnp.exp2(st - sub).astype(BF16)
        acc_ref[:, cols] = alpha * acc_ref[:, cols] + jnp.dot(vt_aug, p, preferred_element_type=F32)
        m_ref[:, cols] = m_new


def _softmax_init(m_ref, acc_ref):
    m_ref[...] = jnp.full(m_ref.shape, -jnp.inf, F32)
    acc_ref[...] = jnp.zeros(acc_ref.shape, F32)


def _softmax_result_t(acc_ref):
    acc = acc_ref[...]
    return acc[0:HEAD_DIM] / acc[HEAD_DIM:HEAD_DIM + 1]


def _transposed_v(z3):
    s = z3.shape[1]
    v = jnp.concatenate([z3[SLAB_VA:SLAB_VA + A_HEADS], z3[SLAB_VB:SLAB_VB + B_KV_HEADS]])
    ones = jnp.ones((v.shape[0], VT_ROWS - HEAD_DIM, s), BF16)
    return jnp.concatenate([jnp.swapaxes(v, 1, 2), ones], axis=1)


def _qk_scores(a, b):
    return lax.dot_general(a, b, (((1,), (1,)), ((), ())), preferred_element_type=F32)


def _gqa_kernel(q_ref, k_ref, vt_ref, o_ref, m_ref, acc_ref):
    ki = pl.program_id(2)
    tq = q_ref.shape[1]

    @pl.when(ki == 0)
    def _():
        _softmax_init(m_ref, acc_ref)

    per_head = tq // QCHUNK

    def q_chunk(j):
        r0 = (j % per_head) * QCHUNK
        return q_ref[j // per_head, r0:r0 + QCHUNK, :]

    _softmax_update_t(k_ref[...], q_chunk, vt_ref[...], m_ref, acc_ref)

    @pl.when(ki == pl.num_programs(2) - 1)
    def _():
        ot = _softmax_result_t(acc_ref)
        for g in range(B_GROUP):
            o_ref[:, g * HEAD_DIM:(g + 1) * HEAD_DIM] = ot[:, g * tq:(g + 1) * tq].T.astype(o_ref.dtype)


def _gqa(qk, vt):
    s = qk.shape[1]
    tq, tk = TQ_GQA, TK_GQA
    cols = B_GROUP * tq
    est = (2 * (2 * _nbytes((cols, LANE), BF16) + _nbytes((tk, LANE), BF16) + _nbytes((VT_ROWS, tk), BF16))
           + 3 * _nbytes((VT_ROWS, cols), F32) + 3 * _nbytes((tk, cols), F32))
    return pl.pallas_call(
        _gqa_kernel,
        grid=(B_KV_HEADS, s // tq, s // tk),
        in_specs=[pl.BlockSpec((B_GROUP, tq, LANE), lambda h, i, j: (h, i, 0)),
                  pl.BlockSpec((None, tk, LANE), lambda h, i, j: (B_HEADS + h, j, 0)),
                  pl.BlockSpec((None, VT_ROWS, tk), lambda h, i, j: (VT_B + h, 0, j))],
        out_specs=pl.BlockSpec((tq, B_GROUP * HEAD_DIM), lambda h, i, j: (i, h)),
        out_shape=jax.ShapeDtypeStruct((s, B_WIDTH), BF16),
        scratch_shapes=[pltpu.VMEM((1, cols), F32), pltpu.VMEM((VT_ROWS, cols), F32)],
        compiler_params=_cparams(("parallel", "parallel", "arbitrary"), est),
        name="gqa_attention",
    )(qk, qk, vt)


def _t5_bias_tile_t(t5_ref, h, key_offset, tk, tq):
    kpos = lax.broadcasted_iota(jnp.int32, (tk, tq), 0) + key_offset
    qpos = lax.broadcasted_iota(jnp.int32, (tk, tq), 1)
    rel = kpos - qpos
    nb = T5_BUCKETS // 2
    max_exact = nb // 2
    ret = jnp.where(rel > 0, nb, 0)
    n = jnp.abs(rel)
    nf = jnp.maximum(n, 1).astype(F32)
    large = max_exact + (jnp.log(nf / max_exact) / math.log(T5_MAX_DIST / max_exact)
                         * (nb - max_exact)).astype(jnp.int32)
    large = jnp.minimum(large, nb - 1)
    bucket = ret + jnp.where(n < max_exact, n, large)
    bias = jnp.full((tk, tq), t5_ref[h], F32)
    for b in range(1, T5_BUCKETS):
        bias = jnp.where(bucket == b, t5_ref[b * A_HEADS + h], bias)
    return bias * LOG2E


def _diff_kernel(t5_ref, lam_ref, q_ref, k_ref, vt_ref, g_ref, o_ref,
                 qs_ref, bias_ref, m_ref, acc_ref):
    h, qi, ki = pl.program_id(0), pl.program_id(1), pl.program_id(2)
    tq, tk = q_ref.shape[0], k_ref.shape[0]
    n_sub = tq // tk
    n_near = n_sub + 2

    @pl.when((qi == 0) & (ki == 0))
    def _():
        for o in range(n_near):
            tile = _t5_bias_tile_t(t5_ref, h, (o - 1) * tk, tk, tq)
            bias_ref[o, :, 0:tq] = tile
            bias_ref[o, :, tq:2 * tq] = tile

    @pl.when(ki == 0)
    def _():
        _softmax_init(m_ref, acc_ref)
        q = q_ref[...]
        lane = lax.broadcasted_iota(jnp.int32, q.shape, 1)
        zero = jnp.zeros_like(q)
        qs_ref[0:tq, :] = jnp.where(lane < A_QK_DIM, q, zero)
        qs_ref[tq:2 * tq, :] = jnp.where(lane >= A_QK_DIM, q, zero)

    o = ki - n_sub * qi + 1

    def q_chunk(j):
        return qs_ref[j * QCHUNK:(j + 1) * QCHUNK, :]

    @pl.when((o >= 0) & (o < n_near))
    def _():
        oc = jnp.clip(o, 0, n_near - 1)
        _softmax_update_t(k_ref[...], q_chunk, vt_ref[...], m_ref, acc_ref,
                          bias_chunk=lambda j: bias_ref[oc, :, j * QCHUNK:(j + 1) * QCHUNK])

    @pl.when(o < 0)
    def _():
        c = t5_ref[(T5_BUCKETS // 2 - 1) * A_HEADS + h] * LOG2E
        _softmax_update_t(k_ref[...], q_chunk, vt_ref[...], m_ref, acc_ref, shift=c)

    @pl.when(o >= n_near)
    def _():
        c = t5_ref[(T5_BUCKETS - 1) * A_HEADS + h] * LOG2E
        _softmax_update_t(k_ref[...], q_chunk, vt_ref[...], m_ref, acc_ref, shift=c)

    @pl.when(ki == pl.num_programs(2) - 1)
    def _():
        ot = _softmax_result_t(acc_ref)
        out = (ot[:, 0:tq] - lam_ref[0] * ot[:, tq:2 * tq]).T
        o_ref[...] = (_rmsnorm_rows(out, g_ref[...]) * lam_ref[1]).astype(o_ref.dtype)


def _diff_attention(z3, vt, t5_flat, lam_pair, subln_g):
    s = z3.shape[1]
    tq, tk = TQ_DIFF, TK_DIFF
    assert tk >= T5_MAX_DIST and tq % tk == 0 and s % tq == 0
    n_near = tq // tk + 2
    est = (2 * (2 * _nbytes((tq, LANE), BF16) + _nbytes((tk, LANE), BF16) + _nbytes((VT_ROWS, tk), BF16))
           + _nbytes((2 * tq, LANE), BF16) + n_near * _nbytes((tk, 2 * tq), F32)
           + 3 * _nbytes((VT_ROWS, 2 * tq), F32) + 3 * _nbytes((tk, 2 * tq), F32))
    grid_spec = pltpu.PrefetchScalarGridSpec(
        num_scalar_prefetch=2,
        grid=(A_HEADS, s // tq, s // tk),
        in_specs=[pl.BlockSpec((None, tq, LANE), lambda h, i, j, *_: (SLAB_QA + h, i, 0)),
                  pl.BlockSpec((None, tk, LANE), lambda h, i, j, *_: (SLAB_KA + h, j, 0)),
                  pl.BlockSpec((None, VT_ROWS, tk), lambda h, i, j, *_: (VT_A + h, 0, j)),
                  pl.BlockSpec((1, LANE), lambda h, i, j, *_: (0, 0))],
        out_specs=pl.BlockSpec((tq, HEAD_DIM), lambda h, i, j, *_: (i, h)),
        scratch_shapes=[pltpu.VMEM((2 * tq, LANE), BF16), pltpu.VMEM((n_near, tk, 2 * tq), F32),
                        pltpu.VMEM((1, 2 * tq), F32), pltpu.VMEM((VT_ROWS, 2 * tq), F32)])
    return pl.pallas_call(
        _diff_kernel,
        grid_spec=grid_spec,
        out_shape=jax.ShapeDtypeStruct((s, A_WIDTH), BF16),
        compiler_params=_cparams(("arbitrary", "arbitrary", "arbitrary"), est),
        name="diff_attention",
    )(t5_flat, lam_pair, z3, z3, vt, subln_g.reshape(1, LANE))


def _na_window_start(first_row, rows):
    return max(0, min(first_row - NA_ROWS // 2, rows - NA_WIN_ROWS))


def _na_build_bias(rpb_ref, h, bias_ref, rows):
    cq = lax.broadcasted_iota(jnp.int32, (GRID_W, LANE), 0)
    ck = lax.broadcasted_iota(jnp.int32, (GRID_W, LANE), 1) % GRID_W
    cs = jnp.clip(cq - NA_COLS // 2, 0, GRID_W - NA_COLS)
    col_valid = (ck >= cs) & (ck < cs + NA_COLS)
    coff = jnp.clip(ck - cq + (NA_COLS - 1), 0, NA_COFF - 1)
    lower_half = lax.broadcasted_iota(jnp.int32, (GRID_W, LANE), 1) < GRID_W
    masked = jnp.full((GRID_W, LANE), MASK_VALUE, F32)
    col_bias = []
    for ro in range(NA_ROFF):
        base = (h * NA_ROFF + ro) * NA_COFF
        b = jnp.full((GRID_W, LANE), rpb_ref[base], F32)
        for c in range(1, NA_COFF):
            b = jnp.where(coff == c, rpb_ref[base + c], b)
        col_bias.append(jnp.where(col_valid, b * LOG2E, MASK_VALUE))
    n_groups = rows // NA_GROUP_ROWS
    for var, group in enumerate((0, 1, n_groups - 1)):
        ws = _na_window_start(group * NA_GROUP_ROWS, rows)
        for a in range(NA_GROUP_ROWS):
            r = group * NA_GROUP_ROWS + a
            rs = max(0, min(r - NA_ROWS // 2, rows - NA_ROWS))
            for ip in range(NA_WIN_ROWS // 2):
                halves = []
                for kr in (ws + 2 * ip, ws + 2 * ip + 1):
                    valid = rs <= kr < rs + NA_ROWS
                    halves.append(col_bias[kr - r + NA_ROWS - 1] if valid else masked)
                bias_ref[var, a * GRID_W:(a + 1) * GRID_W, ip * LANE:(ip + 1) * LANE] = (
                    jnp.where(lower_half, halves[0], halves[1]))


def _na_kernel(rpb_ref, q_ref, k_ref, v_ref, o_ref, bias_ref, *, rows):
    h, g = pl.program_id(0), pl.program_id(1)
    n_groups = rows // NA_GROUP_ROWS
    win = NA_WIN_ROWS * GRID_W

    @pl.when(g == 0)
    def _():
        _na_build_bias(rpb_ref, h, bias_ref, rows)

    ws = jnp.clip(g * NA_GROUP_ROWS - NA_ROWS // 2, 0, rows - NA_WIN_ROWS)
    start = pl.multiple_of(ws * GRID_W, (NA_ROWS // 2) * GRID_W)
    kw = k_ref[pl.ds(start, win), :]
    vw = v_ref[pl.ds(start, win), :]
    var = jnp.where(g == 0, 0, jnp.where(g == n_groups - 1, 2, 1))
    s = _qk_scores(q_ref[...], kw) + bias_ref[var]
    m = jnp.max(s, axis=-1, keepdims=True)
    p = jnp.exp2(s - m)
    l = jnp.sum(p, axis=-1, keepdims=True)
    o = jnp.dot(p.astype(BF16), vw, preferred_element_type=F32) / l
    o_ref[...] = o.astype(o_ref.dtype)


def _neighbourhood_attention(z3, rpb_flat):
    s = z3.shape[1]
    rows = s // GRID_W
    assert rows % NA_GROUP_ROWS == 0 and rows >= 3 * NA_GROUP_ROWS
    tq = NA_GROUP_ROWS * GRID_W
    win = NA_WIN_ROWS * GRID_W
    est = (2 * (2 * _nbytes((tq, LANE), BF16) + 2 * _nbytes((s, LANE), BF16))
           + 3 * _nbytes((tq, win), F32) + 3 * _nbytes((tq, win), F32))
    grid_spec = pltpu.PrefetchScalarGridSpec(
        num_scalar_prefetch=1,
        grid=(C_HEADS, rows // NA_GROUP_ROWS),
        in_specs=[pl.BlockSpec((None, tq, LANE), lambda h, g, *_: (SLAB_QC + h, g, 0)),
                  pl.BlockSpec((None, s, LANE), lambda h, g, *_: (SLAB_KC + h, 0, 0)),
                  pl.BlockSpec((None, s, LANE), lambda h, g, *_: (SLAB_VC + h, 0, 0))],
        out_specs=pl.BlockSpec((tq, HEAD_DIM), lambda h, g, *_: (g, h)),
        scratch_shapes=[pltpu.VMEM((3, tq, win), F32)])
    return pl.pallas_call(
        functools.partial(_na_kernel, rows=rows),
        grid_spec=grid_spec,
        out_shape=jax.ShapeDtypeStruct((s, C_WIDTH), BF16),
        compiler_params=_cparams(("arbitrary", "arbitrary"), est),
        name="neighbourhood_attention",
    )(rpb_flat, z3, z3, z3)


def _router_kernel(x_ref, g_ref, wr_ref, o_ref):
    h = _rmsnorm_rows(x_ref[...], g_ref[...])
    logits = jnp.dot(h, wr_ref[...], preferred_element_type=F32, precision=lax.Precision.HIGHEST)
    lane = lax.broadcasted_iota(jnp.int32, logits.shape, 1)
    lane_f = lane.astype(F32)
    logits = jnp.where(lane < N_EXPERTS, logits, -jnp.inf)
    m1 = jnp.max(logits, axis=-1, keepdims=True)
    i1 = jnp.min(jnp.where(logits == m1, lane_f, float(LANE)), axis=-1, keepdims=True)
    rest = jnp.where(lane_f == i1, -jnp.inf, logits)
    m2 = jnp.max(rest, axis=-1, keepdims=True)
    i2 = jnp.min(jnp.where(rest == m2, lane_f, float(LANE)), axis=-1, keepdims=True)
    e = jnp.exp(m2 - m1)
    g1 = 1.0 / (1.0 + e)
    g2 = e / (1.0 + e)
    o_ref[...] = jnp.where(lane == 0, i1, jnp.where(lane == 1, i2, jnp.where(lane == 2, g1, g2)))


def _router(x, g, w_router):
    s, d = x.shape
    tm = TM_ROUTE
    wr = jnp.zeros((d, LANE), F32).at[:, :N_EXPERTS].set(w_router)
    est = 2 * (_nbytes((tm, d), F32) + _nbytes((d, LANE), F32) + _nbytes((tm, LANE), F32)) + 4 * _nbytes((tm, d), F32)
    return pl.pallas_call(
        _router_kernel,
        grid=(s // tm,),
        in_specs=[pl.BlockSpec((tm, d), lambda i: (i, 0)),
                  pl.BlockSpec((1, d), lambda i: (0, 0)),
                  pl.BlockSpec((d, LANE), lambda i: (0, 0))],
        out_specs=pl.BlockSpec((tm, LANE), lambda i: (i, 0)),
        out_shape=jax.ShapeDtypeStruct((s, LANE), F32),
        compiler_params=_cparams(("parallel",), est),
        name="moe_router",
    )(x, g.reshape(1, d), wr)


def _row_gather(src_hbm, dst_ref, sem, n_rows, row_of):
    def issue(r, carry):
        pltpu.make_async_copy(src_hbm.at[pl.ds(row_of(r), 1)], dst_ref.at[pl.ds(r, 1)], sem).start()
        return carry
    lax.fori_loop(0, n_rows, issue, 0)
    pltpu.make_async_copy(src_hbm.at[pl.ds(0, n_rows)], dst_ref.at[pl.ds(0, n_rows)], sem).wait()


def _moe_kernel(tile_e_ref, nused_ref, tok_ref, x_hbm, gn_ref, wg_ref, wu_ref, wd_ref, bg_ref,
                o_ref, xg_ref, xb_ref, acc_ref, sem):
    i, f = pl.program_id(0), pl.program_id(1)
    tm = xg_ref.shape[0]
    used = i < nused_ref[0]
    last = f == pl.num_programs(1) - 1

    @pl.when(used & (f == 0))
    def _():
        _row_gather(x_hbm, xg_ref, sem, tm, lambda r: tok_ref[i * tm + r])
        xb_ref[...] = _rmsnorm_rows(xg_ref[...], gn_ref[...]).astype(BF16)
        acc_ref[...] = jnp.zeros(acc_ref.shape, F32)

    @pl.when(used)
    def _():
        hb = xb_ref[...]
        g = jnp.dot(hb, wg_ref[...], preferred_element_type=F32)
        u = jnp.dot(hb, wu_ref[...], preferred_element_type=F32)
        act = (g * _sigmoid(g) * u).astype(BF16)
        acc_ref[...] += jnp.dot(act, wd_ref[...], preferred_element_type=F32)

    @pl.when(used & last)
    def _():
        o_ref[...] = acc_ref[...] * bg_ref[...]

    @pl.when(jnp.logical_not(used) & last)
    def _():
        o_ref[...] = jnp.zeros(o_ref.shape, F32)


def _moe_experts(x, gn, wg, wu, wd, tile_e, n_used, buf_tok, buf_g):
    d = x.shape[1]
    fdim = wg.shape[2]
    cap = buf_tok.shape[0]
    tm, tf = TM_MOE, TF_MOE
    nf = fdim // tf

    def fsel(i, f, nused):
        return jnp.where(i < nused[0], f, nf - 1)

    est = (2 * (3 * _nbytes((d, tf), BF16) + _nbytes((tm, d), F32) + _nbytes((tm, LANE), F32))
           + 2 * _nbytes((tm, d), F32) + _nbytes((tm, d), BF16)
           + 3 * _nbytes((tm, tf), F32) + _nbytes((tm, d), F32))
    grid_spec = pltpu.PrefetchScalarGridSpec(
        num_scalar_prefetch=3,
        grid=(cap // tm, nf),
        in_specs=[pl.BlockSpec(memory_space=pl.ANY),
                  pl.BlockSpec((1, d), lambda i, f, te, nu, tok: (0, 0)),
                  pl.BlockSpec((None, d, tf), lambda i, f, te, nu, tok: (te[i], 0, fsel(i, f, nu))),
                  pl.BlockSpec((None, d, tf), lambda i, f, te, nu, tok: (te[i], 0, fsel(i, f, nu))),
                  pl.BlockSpec((None, tf, d), lambda i, f, te, nu, tok: (te[i], fsel(i, f, nu), 0)),
                  pl.BlockSpec((tm, 1), lambda i, f, te, nu, tok: (i, 0))],
        out_specs=pl.BlockSpec((tm, d), lambda i, f, te, nu, tok: (i, 0)),
        scratch_shapes=[pltpu.VMEM((tm, d), F32), pltpu.VMEM((tm, d), BF16),
                        pltpu.VMEM((tm, d), F32), pltpu.SemaphoreType.DMA(())])
    return pl.pallas_call(
        _moe_kernel,
        grid_spec=grid_spec,
        out_shape=jax.ShapeDtypeStruct((cap, d), F32),
        compiler_params=_cparams(("arbitrary", "arbitrary"), est),
        name="moe_experts",
    )(tile_e, n_used, buf_tok, x, gn.reshape(1, d), wg, wu, wd, buf_g.reshape(cap, 1))


def _combine_kernel(pos_ref, x_ref, y_hbm, o_ref, buf_ref, sem):
    i = pl.program_id(0)
    tm = x_ref.shape[0]
    _row_gather(y_hbm, buf_ref, sem, TOP_K * tm,
                lambda r: pos_ref[TOP_K * (i * tm + r % tm) + r // tm])
    o_ref[...] = x_ref[...] + (buf_ref[0:tm, :] + buf_ref[tm:2 * tm, :])


def _moe_combine(x, yb, pos_flat):
    s, d = x.shape
    tm = TM_COMB
    est = 4 * _nbytes((tm, d), F32) + _nbytes((TOP_K * tm, d), F32) + 2 * _nbytes((tm, d), F32)
    grid_spec = pltpu.PrefetchScalarGridSpec(
        num_scalar_prefetch=1,
        grid=(s // tm,),
        in_specs=[pl.BlockSpec((tm, d), lambda i, pos: (i, 0)),
                  pl.BlockSpec(memory_space=pl.ANY)],
        out_specs=pl.BlockSpec((tm, d), lambda i, pos: (i, 0)),
        scratch_shapes=[pltpu.VMEM((TOP_K * tm, d), F32), pltpu.SemaphoreType.DMA(())])
    return pl.pallas_call(
        _combine_kernel,
        grid_spec=grid_spec,
        out_shape=jax.ShapeDtypeStruct((s, d), F32),
        compiler_params=_cparams(("arbitrary",), est),
        name="moe_combine",
    )(pos_flat, x, yb)


def _moe_layer(x, gn, w_router, wg, wu, wd):
    s = x.shape[0]
    tm = TM_MOE
    route = _router(x, gn, w_router)
    e_flat = route[:, 0:TOP_K].astype(jnp.int32).reshape(-1)
    g_flat = route[:, TOP_K:2 * TOP_K].reshape(-1)
    n_assign = s * TOP_K
    onehot = (e_flat[:, None] == jnp.arange(N_EXPERTS, dtype=jnp.int32)[None, :]).astype(jnp.int32)
    csum = jnp.cumsum(onehot, axis=0)
    rank = jnp.sum(csum * onehot, axis=1) - 1
    counts = csum[-1]
    pcounts = (counts + tm - 1) // tm * tm
    pends = jnp.cumsum(pcounts)
    pstarts = pends - pcounts
    dest = jnp.sum(pstarts[None, :] * onehot, axis=1) + rank
    cap = n_assign + N_EXPERTS * tm
    tok_flat = jnp.arange(n_assign, dtype=jnp.int32) // TOP_K
    buf_tok = jnp.zeros((cap,), jnp.int32).at[dest].set(tok_flat)
    buf_g = jnp.zeros((cap,), F32).at[dest].set(g_flat)
    n_tiles = cap // tm
    tile_e = jnp.minimum(jnp.searchsorted(pends, jnp.arange(n_tiles, dtype=jnp.int32) * tm, side='right'),
                         N_EXPERTS - 1).astype(jnp.int32)
    n_used = (pends[-1:] // tm).astype(jnp.int32)
    yb = _moe_experts(x, gn, wg, wu, wd, tile_e, n_used, buf_tok, buf_g)
    return _moe_combine(x, yb, dest.astype(jnp.int32))


def kernel(x, w_in, w_branch_a, w_branch_b, w_branch_c, w_out, norm_mix, norm_ffn, norm_final, t5_bias, diff_lambda, diff_subln, qk_norm_b, na_rpb, ffn_gate, ffn_up, ffn_down, moe_router, moe_gate, moe_up, moe_down):
    bn, s, d = x.shape
    assert bn == 1 and d == D_MODEL
    depth = w_in.shape[0]
    xs = x.reshape(s, d)
    cos, sin_signed = _rope_tables(s)
    t5_flat = t5_bias.astype(F32).reshape(-1)
    colscale = jnp.ones((1, QKV_WIDTH), F32)
    colscale = colscale.at[:, SLAB_QA * LANE:SLAB_KA * LANE].set(A_QK_DIM ** -0.5 * LOG2E)
    colscale = colscale.at[:, SLAB_QC * LANE:SLAB_KC * LANE].set(HEAD_DIM ** -0.5 * LOG2E)

    for l in range(depth):
        h = _rmsnorm(xs, norm_mix[l], BF16)
        w_l = w_in[l].astype(BF16)
        z3 = _proj_slabs(h, w_l[:, :QKV_WIDTH], colscale)
        gates = _matmul_sigmoid(h, w_l[:, QKV_WIDTH:])

        lam_init = 0.8 - 0.6 * math.exp(-0.3 * l)
        lp = diff_lambda[l].astype(F32)
        lam = jnp.exp(jnp.sum(lp[0] * lp[1])) - jnp.exp(jnp.sum(lp[2] * lp[3])) + lam_init
        lam_pair = jnp.stack([lam, jnp.asarray(1.0 - lam_init, F32)]).astype(F32)
        vt = _transposed_v(z3)
        ya = _diff_attention(z3, vt, t5_flat, lam_pair, diff_subln[l].astype(F32))

        q_gain = qk_norm_b[l, 0].astype(F32) * (HEAD_DIM ** -0.5 * LOG2E)
        k_gain = qk_norm_b[l, 1].astype(F32)
        gains = jnp.concatenate([jnp.tile(q_gain[None], (B_HEADS, 1)),
                                 jnp.tile(k_gain[None], (B_KV_HEADS, 1))])[:, None, :]
        yb = _gqa(_qk_prep(z3, gains, cos, sin_signed), vt)

        yc = _neighbourhood_attention(z3, na_rpb[l].astype(F32).reshape(-1))

        merged = _merge(ya, yb, yc, w_branch_a[l].astype(BF16), w_branch_b[l].astype(BF16),
                        w_branch_c[l].astype(BF16), gates)
        xs = _matmul_residual(merged, w_out[l].astype(BF16), xs, TM_OUT, TN_OUT, "out_proj")

        j = l // 2
        if l % 2 == 0:
            hf = _rmsnorm(xs, norm_ffn[l], BF16)
            act = _glu(hf, ffn_gate[j].astype(BF16), ffn_up[j].astype(BF16))
            xs = _matmul_residual(act, ffn_down[j].astype(BF16), xs, TM_DOWN, TN_DOWN, "ffn_down")
        else:
            xs = _moe_layer(xs, norm_ffn[l], moe_router[j], moe_gate[j].astype(BF16),
                            moe_up[j].astype(BF16), moe_down[j].astype(BF16))
    return _rmsnorm(xs, norm_final, x.dtype).reshape(bn, s, d)
```

```python
import functools
import math

import jax
import jax.numpy as jnp
from jax import lax
from jax.experimental import pallas as pl
from jax.experimental.pallas import tpu as pltpu

F32 = jnp.float32
BF16 = jnp.bfloat16

D_MODEL = 2048
GRID_W = 64
HEAD_DIM = 128
NORM_EPS = 1e-6
A_HEADS = 4
A_QK_DIM = 64
A_WIDTH = A_HEADS * HEAD_DIM
B_HEADS = 8
B_KV_HEADS = 2
B_GROUP = B_HEADS // B_KV_HEADS
B_WIDTH = B_HEADS * HEAD_DIM
C_HEADS = 4
C_WIDTH = C_HEADS * HEAD_DIM
NA_ROWS = 8
NA_COLS = 16
NA_ROFF = 2 * NA_ROWS - 1
NA_COFF = 2 * NA_COLS - 1
T5_BUCKETS = 32
T5_MAX_DIST = 128
ROPE_THETA = 10000.0
N_EXPERTS = 8
TOP_K = 2
QKV_WIDTH = 2 * A_WIDTH + A_WIDTH + B_WIDTH + 2 * B_KV_HEADS * HEAD_DIM + 3 * C_WIDTH
LOG2E = 1.4426950408889634
MASK_VALUE = -1e30

LANE = 128
V7X_VMEM_BYTES = 64 * 1024 * 1024
VMEM_CAP_BYTES = 56 * 1024 * 1024
QCHUNK = 512

SLAB_QA, SLAB_KA, SLAB_VA = 0, 4, 8
SLAB_QB, SLAB_KB, SLAB_VB = 12, 20, 22
SLAB_QC, SLAB_KC, SLAB_VC = 24, 28, 32
N_SLABS = QKV_WIDTH // LANE
VT_ROWS = HEAD_DIM + 16

TM_NORM = 512
TM_PROJ, TN_PROJ = 1024, 1536
TS_PREP = 1024
TQ_DIFF, TK_DIFF = 1024, 512
TQ_GQA, TK_GQA = 512, 512
KV_UNROLL = 4
QK_AHEAD = 3
NA_GROUP_ROWS = 8
NA_WIN_ROWS = 16
TM_MERGE, TN_MERGE = 512, 1024
TM_OUT, TN_OUT = 1024, 1024
TM_GLU, TF_GLU = 1024, 512
TM_DOWN, TN_DOWN = 512, 512
TM_ROUTE = 512
TM_MOE, TF_MOE = 512, 512
TM_COMB = 256


def _cparams(semantics, vmem_bytes):
    return pltpu.CompilerParams(dimension_semantics=semantics,
                                vmem_limit_bytes=int(min(vmem_bytes, VMEM_CAP_BYTES)))


def _nbytes(shape, dtype):
    return math.prod(shape) * jnp.dtype(dtype).itemsize


def _rmsnorm_rows(x, g):
    ms = jnp.mean(x * x, axis=-1, keepdims=True)
    return x * lax.rsqrt(ms + NORM_EPS) * g


def _rmsnorm_kernel(x_ref, g_ref, o_ref):
    o_ref[...] = _rmsnorm_rows(x_ref[...], g_ref[...]).astype(o_ref.dtype)


def _rmsnorm(x, g, out_dtype):
    s, d = x.shape
    tm = TM_NORM
    est = 2 * (_nbytes((tm, d), F32) + _nbytes((tm, d), out_dtype)) + 3 * _nbytes((tm, d), F32)
    return pl.pallas_call(
        _rmsnorm_kernel,
        grid=(s // tm,),
        in_specs=[pl.BlockSpec((tm, d), lambda i: (i, 0)),
                  pl.BlockSpec((1, d), lambda i: (0, 0))],
        out_specs=pl.BlockSpec((tm, d), lambda i: (i, 0)),
        out_shape=jax.ShapeDtypeStruct((s, d), out_dtype),
        compiler_params=_cparams(("parallel",), est),
        name="rmsnorm",
    )(x, g.reshape(1, d))


def _proj_slab_kernel(a_ref, w_ref, cs_ref, o_ref):
    acc = jnp.dot(a_ref[...], w_ref[...], preferred_element_type=F32) * cs_ref[...]
    for c in range(o_ref.shape[0]):
        o_ref[c] = acc[:, c * LANE:(c + 1) * LANE].astype(o_ref.dtype)


def _proj_slabs(a, w, colscale):
    s, k = a.shape
    n = w.shape[1]
    tm, tn = TM_PROJ, TN_PROJ
    spt = tn // LANE
    est = (2 * (_nbytes((tm, k), BF16) + _nbytes((k, tn), BF16) + _nbytes((tm, tn), BF16))
           + 2 * _nbytes((tm, tn), F32))
    return pl.pallas_call(
        _proj_slab_kernel,
        grid=(s // tm, n // tn),
        in_specs=[pl.BlockSpec((tm, k), lambda i, j: (i, 0)),
                  pl.BlockSpec((k, tn), lambda i, j: (0, j)),
                  pl.BlockSpec((1, tn), lambda i, j: (0, j))],
        out_specs=pl.BlockSpec((spt, tm, LANE), lambda i, j: (j, i, 0)),
        out_shape=jax.ShapeDtypeStruct((n // LANE, s, LANE), BF16),
        compiler_params=_cparams(("parallel", "arbitrary"), est),
        name="qkv_proj",
    )(a, w, colscale)


def _sigmoid(x):
    return 1.0 / (1.0 + jnp.exp(-x))


def _matmul_sigmoid_kernel(a_ref, w_ref, o_ref):
    acc = jnp.dot(a_ref[...], w_ref[...], preferred_element_type=F32)
    o_ref[...] = _sigmoid(acc).astype(o_ref.dtype)


def _matmul_sigmoid(a, w):
    s, k = a.shape
    n = w.shape[1]
    tm, tn = TM_PROJ, TN_PROJ
    est = (2 * (_nbytes((tm, k), BF16) + _nbytes((k, tn), BF16) + _nbytes((tm, tn), BF16))
           + 2 * _nbytes((tm, tn), F32))
    return pl.pallas_call(
        _matmul_sigmoid_kernel,
        grid=(s // tm, n // tn),
        in_specs=[pl.BlockSpec((tm, k), lambda i, j: (i, 0)),
                  pl.BlockSpec((k, tn), lambda i, j: (0, j))],
        out_specs=pl.BlockSpec((tm, tn), lambda i, j: (i, j)),
        out_shape=jax.ShapeDtypeStruct((s, n), BF16),
        compiler_params=_cparams(("parallel", "arbitrary"), est),
        name="gate_proj",
    )(a, w)


def _matmul_residual_kernel(a_ref, w_ref, r_ref, o_ref):
    o_ref[...] = r_ref[...] + jnp.dot(a_ref[...], w_ref[...], preferred_element_type=F32)


def _matmul_residual(a, w, r, tm, tn, name):
    s, k = a.shape
    n = w.shape[1]
    est = (2 * (_nbytes((tm, k), BF16) + _nbytes((k, tn), BF16) + 2 * _nbytes((tm, tn), F32))
           + _nbytes((tm, tn), F32))
    return pl.pallas_call(
        _matmul_residual_kernel,
        grid=(s // tm, n // tn),
        in_specs=[pl.BlockSpec((tm, k), lambda i, j: (i, 0)),
                  pl.BlockSpec((k, tn), lambda i, j: (0, j)),
                  pl.BlockSpec((tm, tn), lambda i, j: (i, j))],
        out_specs=pl.BlockSpec((tm, tn), lambda i, j: (i, j)),
        out_shape=jax.ShapeDtypeStruct((s, n), F32),
        compiler_params=_cparams(("parallel", "arbitrary"), est),
        name=name,
    )(a, w, r)


def _merge_kernel(ya_ref, yb_ref, yc_ref, wa_ref, wb_ref, wc_ref, ga_ref, gb_ref, gc_ref, o_ref):
    acc = ga_ref[...].astype(F32) * jnp.dot(ya_ref[...], wa_ref[...], preferred_element_type=F32)
    acc += gb_ref[...].astype(F32) * jnp.dot(yb_ref[...], wb_ref[...], preferred_element_type=F32)
    acc += gc_ref[...].astype(F32) * jnp.dot(yc_ref[...], wc_ref[...], preferred_element_type=F32)
    o_ref[...] = acc.astype(o_ref.dtype)


def _merge(ya, yb, yc, wa, wb, wc, gates):
    s = ya.shape[0]
    d = wa.shape[1]
    tm, tn = TM_MERGE, TN_MERGE
    nj = d // tn
    kin = A_WIDTH + B_WIDTH + C_WIDTH
    est = (2 * (_nbytes((tm, kin), BF16) + _nbytes((kin, tn), BF16) + 4 * _nbytes((tm, tn), BF16))
           + 4 * _nbytes((tm, tn), F32))
    return pl.pallas_call(
        _merge_kernel,
        grid=(s // tm, nj),
        in_specs=[pl.BlockSpec((tm, A_WIDTH), lambda i, j: (i, 0)),
                  pl.BlockSpec((tm, B_WIDTH), lambda i, j: (i, 0)),
                  pl.BlockSpec((tm, C_WIDTH), lambda i, j: (i, 0)),
                  pl.BlockSpec((A_WIDTH, tn), lambda i, j: (0, j)),
                  pl.BlockSpec((B_WIDTH, tn), lambda i, j: (0, j)),
                  pl.BlockSpec((C_WIDTH, tn), lambda i, j: (0, j)),
                  pl.BlockSpec((tm, tn), lambda i, j: (i, j)),
                  pl.BlockSpec((tm, tn), lambda i, j: (i, nj + j)),
                  pl.BlockSpec((tm, tn), lambda i, j: (i, 2 * nj + j))],
        out_specs=pl.BlockSpec((tm, tn), lambda i, j: (i, j)),
        out_shape=jax.ShapeDtypeStruct((s, d), BF16),
        compiler_params=_cparams(("parallel", "arbitrary"), est),
        name="branch_merge",
    )(ya, yb, yc, wa, wb, wc, gates, gates, gates)


def _glu_kernel(h_ref, wg_ref, wu_ref, o_ref):
    h = h_ref[...]
    g = jnp.dot(h, wg_ref[...], preferred_element_type=F32)
    u = jnp.dot(h, wu_ref[...], preferred_element_type=F32)
    o_ref[...] = (g * _sigmoid(g) * u).astype(o_ref.dtype)


def _glu(h, wg, wu):
    s, k = h.shape
    f = wg.shape[1]
    tm, tf = TM_GLU, TF_GLU
    est = (2 * (_nbytes((tm, k), BF16) + 2 * _nbytes((k, tf), BF16) + _nbytes((tm, tf), BF16))
           + 3 * _nbytes((tm, tf), F32))
    return pl.pallas_call(
        _glu_kernel,
        grid=(s // tm, f // tf),
        in_specs=[pl.BlockSpec((tm, k), lambda i, j: (i, 0)),
                  pl.BlockSpec((k, tf), lambda i, j: (0, j)),
                  pl.BlockSpec((k, tf), lambda i, j: (0, j))],
        out_specs=pl.BlockSpec((tm, tf), lambda i, j: (i, j)),
        out_shape=jax.ShapeDtypeStruct((s, f), BF16),
        compiler_params=_cparams(("parallel", "arbitrary"), est),
        name="ffn_glu",
    )(h, wg, wu)


def _rope_tables(s):
    half = HEAD_DIM // 2
    pos = jnp.arange(s, dtype=jnp.int32)
    row = (pos // GRID_W).astype(F32)
    col = (pos % GRID_W).astype(F32)
    inv = ROPE_THETA ** (-jnp.arange(0, half, 2, dtype=F32) / half)
    ang_r = row[:, None] * inv[None, :]
    ang_c = col[:, None] * inv[None, :]
    ang = jnp.concatenate([ang_r, ang_r, ang_c, ang_c], axis=-1)
    sign = jnp.tile(jnp.concatenate([-jnp.ones((half // 2,), F32), jnp.ones((half // 2,), F32)]), 2)
    return jnp.cos(ang), jnp.sin(ang) * sign[None, :]


def _qk_prep_kernel(z_ref, g_ref, cos_ref, sin_ref, o_ref):
    y = _rmsnorm_rows(z_ref[...].astype(F32), g_ref[...])
    lane = lax.broadcasted_iota(jnp.int32, y.shape, 1)
    quarter = HEAD_DIM // 4
    first = (lane % (2 * quarter)) < quarter
    partner = jnp.where(first, pltpu.roll(y, HEAD_DIM - quarter, 1), pltpu.roll(y, quarter, 1))
    o_ref[...] = (y * cos_ref[...] + partner * sin_ref[...]).astype(o_ref.dtype)


def _qk_prep(z3, gains, cos, sin_signed):
    s = z3.shape[1]
    ts = TS_PREP
    nsl = B_HEADS + B_KV_HEADS
    est = 2 * (2 * _nbytes((ts, LANE), BF16) + 2 * _nbytes((ts, LANE), F32)) + 6 * _nbytes((ts, LANE), F32)
    return pl.pallas_call(
        _qk_prep_kernel,
        grid=(s // ts, nsl),
        in_specs=[pl.BlockSpec((None, ts, LANE), lambda i, n: (SLAB_QB + n, i, 0)),
                  pl.BlockSpec((None, 1, LANE), lambda i, n: (n, 0, 0)),
                  pl.BlockSpec((ts, LANE), lambda i, n: (i, 0)),
                  pl.BlockSpec((ts, LANE), lambda i, n: (i, 0))],
        out_specs=pl.BlockSpec((None, ts, LANE), lambda i, n: (n, i, 0)),
        out_shape=jax.ShapeDtypeStruct((nsl, s, LANE), BF16),
        compiler_params=_cparams(("parallel", "arbitrary"), est),
        name="gqa_qk_prep",
    )(z3, gains, cos, sin_signed)


def _softmax_update_t(k_tiles, vt_tiles, q_chunk, m_ref, acc_ref, bias_chunk=None):
    n_chunks = m_ref.shape[1] // QCHUNK
    units = [(t, j) for t in range(len(k_tiles)) for j in range(n_chunks)]

    def scores(u):
        t, j = units[u]
        st = _qk_scores(k_tiles[t], q_chunk(j))
        return st if bias_chunk is None else st + bias_chunk(t, j)

    pending = [scores(u) for u in range(min(QK_AHEAD, len(units)))]
    for u, (t, j) in enumerate(units):
        cols = slice(j * QCHUNK, (j + 1) * QCHUNK)
        st = pending.pop(0)
        if u + QK_AHEAD < len(units):
            pending.append(scores(u + QK_AHEAD))
        m_prev = m_ref[:, cols]
        m_new = jnp.maximum(m_prev, jnp.max(st, axis=0, keepdims=True))
        alpha = jnp.exp2(m_prev - m_new)
        p = jnp.exp2(st - m_new).astype(BF16)
        acc_ref[:, cols] = alpha * acc_ref[:, cols] + jnp.dot(vt_tiles[t], p, preferred_element_type=F32)
        m_ref[:, cols] = m_new


def _softmax_init(m_ref, acc_ref):
    m_ref[...] = jnp.full(m_ref.shape, -jnp.inf, F32)
    acc_ref[...] = jnp.zeros(acc_ref.shape, F32)


def _softmax_result_t(acc_ref):
    acc = acc_ref[...]
    return acc[0:HEAD_DIM] / acc[HEAD_DIM:HEAD_DIM + 1]


def _transposed_v(v, tk):
    n, s, _ = v.shape
    vt = jnp.swapaxes(v.reshape(n, s // tk, tk, HEAD_DIM), 2, 3)
    ones = jnp.ones((n, s // tk, VT_ROWS - HEAD_DIM, tk), BF16)
    return jnp.concatenate([vt, ones], axis=2)


def _qk_scores(a, b):
    return lax.dot_general(a, b, (((1,), (1,)), ((), ())), preferred_element_type=F32)


def _gqa_kernel(q_ref, k_ref, vt_ref, o_ref, m_ref, acc_ref):
    tq = q_ref.shape[1]
    n_k, _, tk = vt_ref.shape
    per_head = tq // QCHUNK

    def q_chunk(j):
        r0 = (j % per_head) * QCHUNK
        return q_ref[j // per_head, r0:r0 + QCHUNK, :]

    _softmax_init(m_ref, acc_ref)

    def key_tiles(kg, carry):
        tiles = [kg * KV_UNROLL + t for t in range(KV_UNROLL)]
        _softmax_update_t([k_ref[pl.ds(pl.multiple_of(ki * tk, tk), tk), :] for ki in tiles],
                          [vt_ref[ki] for ki in tiles], q_chunk, m_ref, acc_ref)
        return carry

    lax.fori_loop(0, n_k // KV_UNROLL, key_tiles, 0)
    ot = _softmax_result_t(acc_ref)
    for g in range(B_GROUP):
        o_ref[:, g * HEAD_DIM:(g + 1) * HEAD_DIM] = ot[:, g * tq:(g + 1) * tq].T.astype(o_ref.dtype)


def _gqa(qk, vt):
    s = qk.shape[1]
    tq = TQ_GQA
    n_k, _, tk = vt.shape[1:]
    assert n_k % KV_UNROLL == 0
    cols = B_GROUP * tq
    est = (2 * (2 * _nbytes((cols, LANE), BF16) + _nbytes((s, LANE), BF16) + _nbytes((VT_ROWS, s), BF16))
           + 3 * _nbytes((VT_ROWS, cols), F32) + 4 * _nbytes((tk, QCHUNK), F32))
    return pl.pallas_call(
        _gqa_kernel,
        grid=(B_KV_HEADS, s // tq),
        in_specs=[pl.BlockSpec((B_GROUP, tq, LANE), lambda h, i: (h, i, 0)),
                  pl.BlockSpec((None, s, LANE), lambda h, i: (B_HEADS + h, 0, 0)),
                  pl.BlockSpec((None, n_k, VT_ROWS, tk), lambda h, i: (h, 0, 0, 0))],
        out_specs=pl.BlockSpec((tq, B_GROUP * HEAD_DIM), lambda h, i: (i, h)),
        out_shape=jax.ShapeDtypeStruct((s, B_WIDTH), BF16),
        scratch_shapes=[pltpu.VMEM((1, cols), F32), pltpu.VMEM((VT_ROWS, cols), F32)],
        compiler_params=_cparams(("parallel", "parallel"), est),
        name="gqa_attention",
    )(qk, qk, vt)


def _t5_bias_tile_t(t5_ref, h, key_offset, tk, tq):
    kpos = lax.broadcasted_iota(jnp.int32, (tk, tq), 0) + key_offset
    qpos = lax.broadcasted_iota(jnp.int32, (tk, tq), 1)
    rel = kpos - qpos
    nb = T5_BUCKETS // 2
    max_exact = nb // 2
    ret = jnp.where(rel > 0, nb, 0)
    n = jnp.abs(rel)
    nf = jnp.maximum(n, 1).astype(F32)
    large = max_exact + (jnp.log(nf / max_exact) / math.log(T5_MAX_DIST / max_exact)
                         * (nb - max_exact)).astype(jnp.int32)
    large = jnp.minimum(large, nb - 1)
    bucket = ret + jnp.where(n < max_exact, n, large)
    bias = jnp.full((tk, tq), t5_ref[h], F32)
    for b in range(1, T5_BUCKETS):
        bias = jnp.where(bucket == b, t5_ref[b * A_HEADS + h], bias)
    return bias * LOG2E


def _diff_kernel(t5_ref, lam_ref, q_ref, k_ref, vt_ref, g_ref, o_ref, qs_ref, bias_ref, m_ref, acc_ref):
    h, qi = pl.program_id(0), pl.program_id(1)
    tq = q_ref.shape[0]
    n_k, _, tk = vt_ref.shape
    n_sub = tq // tk
    n_near = n_sub + 2
    per_map = tq // QCHUNK

    @pl.when(qi == 0)
    def _():
        far_left = t5_ref[(T5_BUCKETS // 2 - 1) * A_HEADS + h] * LOG2E
        far_right = t5_ref[(T5_BUCKETS - 1) * A_HEADS + h] * LOG2E
        bias_ref[0] = jnp.full((tk, tq), far_left, F32)
        for o in range(n_near):
            bias_ref[o + 1] = _t5_bias_tile_t(t5_ref, h, (o - 1) * tk, tk, tq)
        bias_ref[n_near + 1] = jnp.full((tk, tq), far_right, F32)

    _softmax_init(m_ref, acc_ref)
    q = q_ref[...]
    lane = lax.broadcasted_iota(jnp.int32, q.shape, 1)
    zero = jnp.zeros_like(q)
    qs_ref[0:tq, :] = jnp.where(lane < A_QK_DIM, q, zero)
    qs_ref[tq:2 * tq, :] = jnp.where(lane >= A_QK_DIM, q, zero)

    def q_chunk(j):
        return qs_ref[j * QCHUNK:(j + 1) * QCHUNK, :]

    def key_tiles(kg, carry):
        tiles = [kg * KV_UNROLL + t for t in range(KV_UNROLL)]
        bias_idx = [jnp.clip(ki - n_sub * qi + 2, 0, n_near + 1) for ki in tiles]

        def bias_chunk(t, j):
            c0 = (j % per_map) * QCHUNK
            return bias_ref[bias_idx[t], :, c0:c0 + QCHUNK]

        _softmax_update_t([k_ref[pl.ds(pl.multiple_of(ki * tk, tk), tk), :] for ki in tiles],
                          [vt_ref[ki] for ki in tiles], q_chunk, m_ref, acc_ref, bias_chunk=bias_chunk)
        return carry

    lax.fori_loop(0, n_k // KV_UNROLL, key_tiles, 0)
    ot = _softmax_result_t(acc_ref)
    out = (ot[:, 0:tq] - lam_ref[0] * ot[:, tq:2 * tq]).T
    o_ref[...] = (_rmsnorm_rows(out, g_ref[...]) * lam_ref[1]).astype(o_ref.dtype)


def _diff_attention(z3, vt, t5_flat, lam_pair, subln_g):
    s = z3.shape[1]
    tq = TQ_DIFF
    n_k, _, tk = vt.shape[1:]
    assert tk >= T5_MAX_DIST and tq % tk == 0 and s % tq == 0 and tq % QCHUNK == 0 and n_k % KV_UNROLL == 0
    n_near = tq // tk + 2
    est = (2 * (2 * _nbytes((tq, LANE), BF16) + _nbytes((s, LANE), BF16) + _nbytes((VT_ROWS, s), BF16))
           + _nbytes((2 * tq, LANE), BF16) + (n_near + 2) * _nbytes((tk, tq), F32)
           + 3 * _nbytes((VT_ROWS, 2 * tq), F32) + 4 * _nbytes((tk, QCHUNK), F32))
    grid_spec = pltpu.PrefetchScalarGridSpec(
        num_scalar_prefetch=2,
        grid=(A_HEADS, s // tq),
        in_specs=[pl.BlockSpec((None, tq, LANE), lambda h, i, *_: (SLAB_QA + h, i, 0)),
                  pl.BlockSpec((None, s, LANE), lambda h, i, *_: (SLAB_KA + h, 0, 0)),
                  pl.BlockSpec((None, n_k, VT_ROWS, tk), lambda h, i, *_: (h, 0, 0, 0)),
                  pl.BlockSpec((1, LANE), lambda h, i, *_: (0, 0))],
        out_specs=pl.BlockSpec((tq, HEAD_DIM), lambda h, i, *_: (i, h)),
        scratch_shapes=[pltpu.VMEM((2 * tq, LANE), BF16), pltpu.VMEM((n_near + 2, tk, tq), F32),
                        pltpu.VMEM((1, 2 * tq), F32), pltpu.VMEM((VT_ROWS, 2 * tq), F32)])
    return pl.pallas_call(
        _diff_kernel,
        grid_spec=grid_spec,
        out_shape=jax.ShapeDtypeStruct((s, A_WIDTH), BF16),
        compiler_params=_cparams(("arbitrary", "arbitrary"), est),
        name="diff_attention",
    )(t5_flat, lam_pair, z3, z3, vt, subln_g.reshape(1, LANE))


def _na_window_start(first_row, rows):
    return max(0, min(first_row - NA_ROWS // 2, rows - NA_WIN_ROWS))


def _na_build_bias(rpb_ref, h, bias_ref, rows):
    cq = lax.broadcasted_iota(jnp.int32, (GRID_W, LANE), 0)
    ck = lax.broadcasted_iota(jnp.int32, (GRID_W, LANE), 1) % GRID_W
    cs = jnp.clip(cq - NA_COLS // 2, 0, GRID_W - NA_COLS)
    col_valid = (ck >= cs) & (ck < cs + NA_COLS)
    coff = jnp.clip(ck - cq + (NA_COLS - 1), 0, NA_COFF - 1)
    lower_half = lax.broadcasted_iota(jnp.int32, (GRID_W, LANE), 1) < GRID_W
    masked = jnp.full((GRID_W, LANE), MASK_VALUE, F32)
    col_bias = []
    for ro in range(NA_ROFF):
        base = (h * NA_ROFF + ro) * NA_COFF
        b = jnp.full((GRID_W, LANE), rpb_ref[base], F32)
        for c in range(1, NA_COFF):
            b = jnp.where(coff == c, rpb_ref[base + c], b)
        col_bias.append(jnp.where(col_valid, b * LOG2E, MASK_VALUE))
    n_groups = rows // NA_GROUP_ROWS
    for var, group in enumerate((0, 1, n_groups - 1)):
        ws = _na_window_start(group * NA_GROUP_ROWS, rows)
        for a in range(NA_GROUP_ROWS):
            r = group * NA_GROUP_ROWS + a
            rs = max(0, min(r - NA_ROWS // 2, rows - NA_ROWS))
            for ip in range(NA_WIN_ROWS // 2):
                halves = []
                for kr in (ws + 2 * ip, ws + 2 * ip + 1):
                    valid = rs <= kr < rs + NA_ROWS
                    halves.append(col_bias[kr - r + NA_ROWS - 1] if valid else masked)
                bias_ref[var, a * GRID_W:(a + 1) * GRID_W, ip * LANE:(ip + 1) * LANE] = (
                    jnp.where(lower_half, halves[0], halves[1]))


def _na_kernel(rpb_ref, q_ref, k_ref, v_ref, o_ref, bias_ref, *, rows):
    h, g = pl.program_id(0), pl.program_id(1)
    n_groups = rows // NA_GROUP_ROWS
    win = NA_WIN_ROWS * GRID_W

    @pl.when(g == 0)
    def _():
        _na_build_bias(rpb_ref, h, bias_ref, rows)

    ws = jnp.clip(g * NA_GROUP_ROWS - NA_ROWS // 2, 0, rows - NA_WIN_ROWS)
    start = pl.multiple_of(ws * GRID_W, (NA_ROWS // 2) * GRID_W)
    kw = k_ref[pl.ds(start, win), :]
    vw = v_ref[pl.ds(start, win), :]
    var = jnp.where(g == 0, 0, jnp.where(g == n_groups - 1, 2, 1))
    s = _qk_scores(q_ref[...], kw) + bias_ref[var]
    m = jnp.max(s, axis=-1, keepdims=True)
    p = jnp.exp2(s - m)
    l = jnp.sum(p, axis=-1, keepdims=True)
    o = jnp.dot(p.astype(BF16), vw, preferred_element_type=F32) / l
    o_ref[...] = o.astype(o_ref.dtype)


def _neighbourhood_attention(z3, rpb_flat):
    s = z3.shape[1]
    rows = s // GRID_W
    assert rows % NA_GROUP_ROWS == 0 and rows >= 3 * NA_GROUP_ROWS
    tq = NA_GROUP_ROWS * GRID_W
    win = NA_WIN_ROWS * GRID_W
    est = (2 * (2 * _nbytes((tq, LANE), BF16) + 2 * _nbytes((s, LANE), BF16))
           + 3 * _nbytes((tq, win), F32) + 3 * _nbytes((tq, win), F32))
    grid_spec = pltpu.PrefetchScalarGridSpec(
        num_scalar_prefetch=1,
        grid=(C_HEADS, rows // NA_GROUP_ROWS),
        in_specs=[pl.BlockSpec((None, tq, LANE), lambda h, g, *_: (SLAB_QC + h, g, 0)),
                  pl.BlockSpec((None, s, LANE), lambda h, g, *_: (SLAB_KC + h, 0, 0)),
                  pl.BlockSpec((None, s, LANE), lambda h, g, *_: (SLAB_VC + h, 0, 0))],
        out_specs=pl.BlockSpec((tq, HEAD_DIM), lambda h, g, *_: (g, h)),
        scratch_shapes=[pltpu.VMEM((3, tq, win), F32)])
    return pl.pallas_call(
        functools.partial(_na_kernel, rows=rows),
        grid_spec=grid_spec,
        out_shape=jax.ShapeDtypeStruct((s, C_WIDTH), BF16),
        compiler_params=_cparams(("arbitrary", "arbitrary"), est),
        name="neighbourhood_attention",
    )(rpb_flat, z3, z3, z3)


def _router_kernel(x_ref, g_ref, wr_ref, o_ref):
    h = _rmsnorm_rows(x_ref[...], g_ref[...])
    logits = jnp.dot(h, wr_ref[...], preferred_element_type=F32, precision=lax.Precision.HIGHEST)
    lane = lax.broadcasted_iota(jnp.int32, logits.shape, 1)
    lane_f = lane.astype(F32)
    logits = jnp.where(lane < N_EXPERTS, logits, -jnp.inf)
    m1 = jnp.max(logits, axis=-1, keepdims=True)
    i1 = jnp.min(jnp.where(logits == m1, lane_f, float(LANE)), axis=-1, keepdims=True)
    rest = jnp.where(lane_f == i1, -jnp.inf, logits)
    m2 = jnp.max(rest, axis=-1, keepdims=True)
    i2 = jnp.min(jnp.where(rest == m2, lane_f, float(LANE)), axis=-1, keepdims=True)
    e = jnp.exp(m2 - m1)
    g1 = 1.0 / (1.0 + e)
    g2 = e / (1.0 + e)
    o_ref[...] = jnp.where(lane == 0, i1, jnp.where(lane == 1, i2, jnp.where(lane == 2, g1, g2)))


def _router(x, g, w_router):
    s, d = x.shape
    tm = TM_ROUTE
    wr = jnp.zeros((d, LANE), F32).at[:, :N_EXPERTS].set(w_router)
    est = 2 * (_nbytes((tm, d), F32) + _nbytes((d, LANE), F32) + _nbytes((tm, LANE), F32)) + 4 * _nbytes((tm, d), F32)
    return pl.pallas_call(
        _router_kernel,
        grid=(s // tm,),
        in_specs=[pl.BlockSpec((tm, d), lambda i: (i, 0)),
                  pl.BlockSpec((1, d), lambda i: (0, 0)),
                  pl.BlockSpec((d, LANE), lambda i: (0, 0))],
        out_specs=pl.BlockSpec((tm, LANE), lambda i: (i, 0)),
        out_shape=jax.ShapeDtypeStruct((s, LANE), F32),
        compiler_params=_cparams(("parallel",), est),
        name="moe_router",
    )(x, g.reshape(1, d), wr)


def _row_gather(src_hbm, dst_ref, sem, n_rows, row_of):
    def issue(r, carry):
        pltpu.make_async_copy(src_hbm.at[pl.ds(row_of(r), 1)], dst_ref.at[pl.ds(r, 1)], sem).start()
        return carry
    lax.fori_loop(0, n_rows, issue, 0)
    pltpu.make_async_copy(src_hbm.at[pl.ds(0, n_rows)], dst_ref.at[pl.ds(0, n_rows)], sem).wait()


def _moe_kernel(tile_e_ref, nused_ref, tok_ref, x_hbm, gn_ref, wg_ref, wu_ref, wd_ref, bg_ref,
                o_ref, xg_ref, xb_ref, acc_ref, sem):
    i, f = pl.program_id(0), pl.program_id(1)
    tm = xg_ref.shape[0]
    used = i < nused_ref[0]
    last = f == pl.num_programs(1) - 1

    @pl.when(used & (f == 0))
    def _():
        _row_gather(x_hbm, xg_ref, sem, tm, lambda r: tok_ref[i * tm + r])
        xb_ref[...] = _rmsnorm_rows(xg_ref[...], gn_ref[...]).astype(BF16)
        acc_ref[...] = jnp.zeros(acc_ref.shape, F32)

    @pl.when(used)
    def _():
        hb = xb_ref[...]
        g = jnp.dot(hb, wg_ref[...], preferred_element_type=F32)
        u = jnp.dot(hb, wu_ref[...], preferred_element_type=F32)
        act = (g * _sigmoid(g) * u).astype(BF16)
        acc_ref[...] += jnp.dot(act, wd_ref[...], preferred_element_type=F32)

    @pl.when(used & last)
    def _():
        o_ref[...] = acc_ref[...] * bg_ref[...]

    @pl.when(jnp.logical_not(used) & last)
    def _():
        o_ref[...] = jnp.zeros(o_ref.shape, F32)


def _moe_experts(x, gn, wg, wu, wd, tile_e, n_used, buf_tok, buf_g):
    d = x.shape[1]
    fdim = wg.shape[2]
    cap = buf_tok.shape[0]
    tm, tf = TM_MOE, TF_MOE
    nf = fdim // tf

    def fsel(i, f, nused):
        return jnp.where(i < nused[0], f, nf - 1)

    est = (2 * (3 * _nbytes((d, tf), BF16) + _nbytes((tm, d), F32) + _nbytes((tm, LANE), F32))
           + 2 * _nbytes((tm, d), F32) + _nbytes((tm, d), BF16)
           + 3 * _nbytes((tm, tf), F32) + _nbytes((tm, d), F32))
    grid_spec = pltpu.PrefetchScalarGridSpec(
        num_scalar_prefetch=3,
        grid=(cap // tm, nf),
        in_specs=[pl.BlockSpec(memory_space=pl.ANY),
                  pl.BlockSpec((1, d), lambda i, f, te, nu, tok: (0, 0)),
                  pl.BlockSpec((None, d, tf), lambda i, f, te, nu, tok: (te[i], 0, fsel(i, f, nu))),
                  pl.BlockSpec((None, d, tf), lambda i, f, te, nu, tok: (te[i], 0, fsel(i, f, nu))),
                  pl.BlockSpec((None, tf, d), lambda i, f, te, nu, tok: (te[i], fsel(i, f, nu), 0)),
                  pl.BlockSpec((tm, 1), lambda i, f, te, nu, tok: (i, 0))],
        out_specs=pl.BlockSpec((tm, d), lambda i, f, te, nu, tok: (i, 0)),
        scratch_shapes=[pltpu.VMEM((tm, d), F32), pltpu.VMEM((tm, d), BF16),
                        pltpu.VMEM((tm, d), F32), pltpu.SemaphoreType.DMA(())])
    return pl.pallas_call(
        _moe_kernel,
        grid_spec=grid_spec,
        out_shape=jax.ShapeDtypeStruct((cap, d), F32),
        compiler_params=_cparams(("arbitrary", "arbitrary"), est),
        name="moe_experts",
    )(tile_e, n_used, buf_tok, x, gn.reshape(1, d), wg, wu, wd, buf_g.reshape(cap, 1))


def _combine_kernel(pos_ref, x_ref, y_hbm, o_ref, buf_ref, sem):
    i = pl.program_id(0)
    tm = x_ref.shape[0]
    _row_gather(y_hbm, buf_ref, sem, TOP_K * tm,
                lambda r: pos_ref[TOP_K * (i * tm + r % tm) + r // tm])
    o_ref[...] = x_ref[...] + (buf_ref[0:tm, :] + buf_ref[tm:2 * tm, :])


def _moe_combine(x, yb, pos_flat):
    s, d = x.shape
    tm = TM_COMB
    est = 4 * _nbytes((tm, d), F32) + _nbytes((TOP_K * tm, d), F32) + 2 * _nbytes((tm, d), F32)
    grid_spec = pltpu.PrefetchScalarGridSpec(
        num_scalar_prefetch=1,
        grid=(s // tm,),
        in_specs=[pl.BlockSpec((tm, d), lambda i, pos: (i, 0)),
                  pl.BlockSpec(memory_space=pl.ANY)],
        out_specs=pl.BlockSpec((tm, d), lambda i, pos: (i, 0)),
        scratch_shapes=[pltpu.VMEM((TOP_K * tm, d), F32), pltpu.SemaphoreType.DMA(())])
    return pl.pallas_call(
        _combine_kernel,
        grid_spec=grid_spec,
        out_shape=jax.ShapeDtypeStruct((s, d), F32),
        compiler_params=_cparams(("arbitrary",), est),
        name="moe_combine",
    )(pos_flat, x, yb)


def _moe_layer(x, gn, w_router, wg, wu, wd):
    s = x.shape[0]
    tm = TM_MOE
    route = _router(x, gn, w_router)
    e_flat = route[:, 0:TOP_K].astype(jnp.int32).reshape(-1)
    g_flat = route[:, TOP_K:2 * TOP_K].reshape(-1)
    n_assign = s * TOP_K
    onehot = (e_flat[:, None] == jnp.arange(N_EXPERTS, dtype=jnp.int32)[None, :]).astype(jnp.int32)
    csum = jnp.cumsum(onehot, axis=0)
    rank = jnp.sum(csum * onehot, axis=1) - 1
    counts = csum[-1]
    pcounts = (counts + tm - 1) // tm * tm
    pends = jnp.cumsum(pcounts)
    pstarts = pends - pcounts
    dest = jnp.sum(pstarts[None, :] * onehot, axis=1) + rank
    cap = n_assign + N_EXPERTS * tm
    tok_flat = jnp.arange(n_assign, dtype=jnp.int32) // TOP_K
    buf_tok = jnp.zeros((cap,), jnp.int32).at[dest].set(tok_flat)
    buf_g = jnp.zeros((cap,), F32).at[dest].set(g_flat)
    n_tiles = cap // tm
    tile_e = jnp.minimum(jnp.searchsorted(pends, jnp.arange(n_tiles, dtype=jnp.int32) * tm, side='right'),
                         N_EXPERTS - 1).astype(jnp.int32)
    n_used = (pends[-1:] // tm).astype(jnp.int32)
    yb = _moe_experts(x, gn, wg, wu, wd, tile_e, n_used, buf_tok, buf_g)
    return _moe_combine(x, yb, dest.astype(jnp.int32))


def kernel(x, w_in, w_branch_a, w_branch_b, w_branch_c, w_out, norm_mix, norm_ffn, norm_final, t5_bias, diff_lambda, diff_subln, qk_norm_b, na_rpb, ffn_gate, ffn_up, ffn_down, moe_router, moe_gate, moe_up, moe_down):
    bn, s, d = x.shape
    assert bn == 1 and d == D_MODEL
    depth = w_in.shape[0]
    xs = x.reshape(s, d)
    cos, sin_signed = _rope_tables(s)
    t5_flat = t5_bias.astype(F32).reshape(-1)
    colscale = jnp.ones((1, QKV_WIDTH), F32)
    colscale = colscale.at[:, SLAB_QA * LANE:SLAB_KA * LANE].set(A_QK_DIM ** -0.5 * LOG2E)
    colscale = colscale.at[:, SLAB_QC * LANE:SLAB_KC * LANE].set(HEAD_DIM ** -0.5 * LOG2E)

    for l in range(depth):
        h = _rmsnorm(xs, norm_mix[l], BF16)
        w_l = w_in[l].astype(BF16)
        z3 = _proj_slabs(h, w_l[:, :QKV_WIDTH], colscale)
        gates = _matmul_sigmoid(h, w_l[:, QKV_WIDTH:])

        lam_init = 0.8 - 0.6 * math.exp(-0.3 * l)
        lp = diff_lambda[l].astype(F32)
        lam = jnp.exp(jnp.sum(lp[0] * lp[1])) - jnp.exp(jnp.sum(lp[2] * lp[3])) + lam_init
        lam_pair = jnp.stack([lam, jnp.asarray(1.0 - lam_init, F32)]).astype(F32)
        vt_a = _transposed_v(z3[SLAB_VA:SLAB_VA + A_HEADS], TK_DIFF)
        vt_b = _transposed_v(z3[SLAB_VB:SLAB_VB + B_KV_HEADS], TK_GQA)
        ya = _diff_attention(z3, vt_a, t5_flat, lam_pair, diff_subln[l].astype(F32))

        q_gain = qk_norm_b[l, 0].astype(F32) * (HEAD_DIM ** -0.5 * LOG2E)
        k_gain = qk_norm_b[l, 1].astype(F32)
        gains = jnp.concatenate([jnp.tile(q_gain[None], (B_HEADS, 1)),
                                 jnp.tile(k_gain[None], (B_KV_HEADS, 1))])[:, None, :]
        yb = _gqa(_qk_prep(z3, gains, cos, sin_signed), vt_b)

        yc = _neighbourhood_attention(z3, na_rpb[l].astype(F32).reshape(-1))

        merged = _merge(ya, yb, yc, w_branch_a[l].astype(BF16), w_branch_b[l].astype(BF16),
                        w_branch_c[l].astype(BF16), gates)
        xs = _matmul_residual(merged, w_out[l].astype(BF16), xs, TM_OUT, TN_OUT, "out_proj")

        j = l // 2
        if l % 2 == 0:
            hf = _rmsnorm(xs, norm_ffn[l], BF16)
            act = _glu(hf, ffn_gate[j].astype(BF16), ffn_up[j].astype(BF16))
            xs = _matmul_residual(act, ffn_down[j].astype(BF16), xs, TM_DOWN, TN_DOWN, "ffn_down")
        else:
            xs = _moe_layer(xs, norm_ffn[l], moe_router[j], moe_gate[j].astype(BF16),
                            moe_up[j].astype(BF16), moe_down[j].astype(BF16))
    return _rmsnorm(xs, norm_final, x.dtype).reshape(bn, s, d)
```

```python
import functools
import math

import jax
import jax.numpy as jnp
from jax import lax
from jax.experimental import pallas as pl
from jax.experimental.pallas import tpu as pltpu

F32 = jnp.float32
BF16 = jnp.bfloat16

D_MODEL = 2048
GRID_W = 64
HEAD_DIM = 128
NORM_EPS = 1e-6
A_HEADS = 4
A_QK_DIM = 64
A_WIDTH = A_HEADS * HEAD_DIM
B_HEADS = 8
B_KV_HEADS = 2
B_GROUP = B_HEADS // B_KV_HEADS
B_WIDTH = B_HEADS * HEAD_DIM
C_HEADS = 4
C_WIDTH = C_HEADS * HEAD_DIM
NA_ROWS = 8
NA_COLS = 16
NA_ROFF = 2 * NA_ROWS - 1
NA_COFF = 2 * NA_COLS - 1
T5_BUCKETS = 32
T5_MAX_DIST = 128
ROPE_THETA = 10000.0
N_EXPERTS = 8
TOP_K = 2
QKV_WIDTH = 2 * A_WIDTH + A_WIDTH + B_WIDTH + 2 * B_KV_HEADS * HEAD_DIM + 3 * C_WIDTH
LOG2E = 1.4426950408889634
MASK_VALUE = -1e30

LANE = 128
V7X_VMEM_BYTES = 64 * 1024 * 1024
VMEM_CAP_BYTES = 56 * 1024 * 1024
QCHUNK = 256

SLAB_QA, SLAB_KA, SLAB_VA = 0, 4, 8
SLAB_QB, SLAB_KB, SLAB_VB = 12, 20, 22
SLAB_QC, SLAB_KC, SLAB_VC = 24, 28, 32
N_SLABS = QKV_WIDTH // LANE
VT_ROWS = HEAD_DIM + 16

TM_NORM = 512
TM_PROJ, TN_PROJ = 1024, 1536
TS_PREP = 1024
TQ_DIFF, TK_DIFF = 1024, 512
TQ_GQA, TK_GQA = 512, 512
KV_UNROLL = 4
QK_AHEAD = 3
NA_GROUP_ROWS = 8
NA_WIN_ROWS = 16
TM_MERGE, TN_MERGE = 512, 1024
TM_OUT, TN_OUT = 1024, 1024
TM_GLU, TF_GLU = 1024, 512
TM_DOWN, TN_DOWN = 512, 512
TM_ROUTE = 512
TM_MOE, TF_MOE = 512, 512
TM_COMB = 256


def _cparams(semantics, vmem_bytes):
    return pltpu.CompilerParams(dimension_semantics=semantics,
                                vmem_limit_bytes=int(min(vmem_bytes, VMEM_CAP_BYTES)))


def _nbytes(shape, dtype):
    return math.prod(shape) * jnp.dtype(dtype).itemsize


def _rmsnorm_rows(x, g):
    ms = jnp.mean(x * x, axis=-1, keepdims=True)
    return x * lax.rsqrt(ms + NORM_EPS) * g


def _rmsnorm_kernel(x_ref, g_ref, o_ref):
    o_ref[...] = _rmsnorm_rows(x_ref[...], g_ref[...]).astype(o_ref.dtype)


def _rmsnorm(x, g, out_dtype):
    s, d = x.shape
    tm = TM_NORM
    est = 2 * (_nbytes((tm, d), F32) + _nbytes((tm, d), out_dtype)) + 3 * _nbytes((tm, d), F32)
    return pl.pallas_call(
        _rmsnorm_kernel,
        grid=(s // tm,),
        in_specs=[pl.BlockSpec((tm, d), lambda i: (i, 0)),
                  pl.BlockSpec((1, d), lambda i: (0, 0))],
        out_specs=pl.BlockSpec((tm, d), lambda i: (i, 0)),
        out_shape=jax.ShapeDtypeStruct((s, d), out_dtype),
        compiler_params=_cparams(("parallel",), est),
        name="rmsnorm",
    )(x, g.reshape(1, d))


def _proj_slab_kernel(a_ref, w_ref, cs_ref, o_ref):
    acc = jnp.dot(a_ref[...], w_ref[...], preferred_element_type=F32) * cs_ref[...]
    for c in range(o_ref.shape[0]):
        o_ref[c] = acc[:, c * LANE:(c + 1) * LANE].astype(o_ref.dtype)


def _proj_slabs(a, w, colscale):
    s, k = a.shape
    n = colscale.shape[1]
    tm, tn = TM_PROJ, TN_PROJ
    spt = tn // LANE
    est = (2 * (_nbytes((tm, k), BF16) + _nbytes((k, tn), BF16) + _nbytes((tm, tn), BF16))
           + 2 * _nbytes((tm, tn), F32))
    return pl.pallas_call(
        _proj_slab_kernel,
        grid=(s // tm, n // tn),
        in_specs=[pl.BlockSpec((tm, k), lambda i, j: (i, 0)),
                  pl.BlockSpec((k, tn), lambda i, j: (0, j)),
                  pl.BlockSpec((1, tn), lambda i, j: (0, j))],
        out_specs=pl.BlockSpec((spt, tm, LANE), lambda i, j: (j, i, 0)),
        out_shape=jax.ShapeDtypeStruct((n // LANE, s, LANE), BF16),
        compiler_params=_cparams(("parallel", "arbitrary"), est),
        name="qkv_proj",
    )(a, w, colscale)


def _sigmoid(x):
    return 1.0 / (1.0 + jnp.exp(-x))


def _matmul_sigmoid_kernel(a_ref, w_ref, o_ref):
    acc = jnp.dot(a_ref[...], w_ref[...], preferred_element_type=F32)
    o_ref[...] = _sigmoid(acc).astype(o_ref.dtype)


def _matmul_sigmoid(a, w, col0):
    s, k = a.shape
    tm, tn = TM_PROJ, TN_PROJ
    n = w.shape[1] - col0
    assert col0 % tn == 0 and n % tn == 0
    j0 = col0 // tn
    est = (2 * (_nbytes((tm, k), BF16) + _nbytes((k, tn), BF16) + _nbytes((tm, tn), BF16))
           + 2 * _nbytes((tm, tn), F32))
    return pl.pallas_call(
        _matmul_sigmoid_kernel,
        grid=(s // tm, n // tn),
        in_specs=[pl.BlockSpec((tm, k), lambda i, j: (i, 0)),
                  pl.BlockSpec((k, tn), lambda i, j: (0, j0 + j))],
        out_specs=pl.BlockSpec((tm, tn), lambda i, j: (i, j)),
        out_shape=jax.ShapeDtypeStruct((s, n), BF16),
        compiler_params=_cparams(("parallel", "arbitrary"), est),
        name="gate_proj",
    )(a, w)


def _matmul_residual_kernel(a_ref, w_ref, r_ref, o_ref):
    o_ref[...] = r_ref[...] + jnp.dot(a_ref[...], w_ref[...], preferred_element_type=F32)


def _matmul_residual(a, w, r, tm, tn, name):
    s, k = a.shape
    n = w.shape[1]
    est = (2 * (_nbytes((tm, k), BF16) + _nbytes((k, tn), BF16) + 2 * _nbytes((tm, tn), F32))
           + _nbytes((tm, tn), F32))
    return pl.pallas_call(
        _matmul_residual_kernel,
        grid=(s // tm, n // tn),
        in_specs=[pl.BlockSpec((tm, k), lambda i, j: (i, 0)),
                  pl.BlockSpec((k, tn), lambda i, j: (0, j)),
                  pl.BlockSpec((tm, tn), lambda i, j: (i, j))],
        out_specs=pl.BlockSpec((tm, tn), lambda i, j: (i, j)),
        out_shape=jax.ShapeDtypeStruct((s, n), F32),
        compiler_params=_cparams(("parallel", "arbitrary"), est),
        name=name,
    )(a, w, r)


def _merge_kernel(ya_ref, yb_ref, yc_ref, wa_ref, wb_ref, wc_ref, ga_ref, gb_ref, gc_ref, o_ref):
    acc = ga_ref[...].astype(F32) * jnp.dot(ya_ref[...], wa_ref[...], preferred_element_type=F32)
    acc += gb_ref[...].astype(F32) * jnp.dot(yb_ref[...], wb_ref[...], preferred_element_type=F32)
    acc += gc_ref[...].astype(F32) * jnp.dot(yc_ref[...], wc_ref[...], preferred_element_type=F32)
    o_ref[...] = acc.astype(o_ref.dtype)


def _merge(ya, yb, yc, wa, wb, wc, gates):
    s = ya.shape[0]
    d = wa.shape[1]
    tm, tn = TM_MERGE, TN_MERGE
    nj = d // tn
    kin = A_WIDTH + B_WIDTH + C_WIDTH
    est = (2 * (_nbytes((tm, kin), BF16) + _nbytes((kin, tn), BF16) + 4 * _nbytes((tm, tn), BF16))
           + 4 * _nbytes((tm, tn), F32))
    return pl.pallas_call(
        _merge_kernel,
        grid=(s // tm, nj),
        in_specs=[pl.BlockSpec((tm, A_WIDTH), lambda i, j: (i, 0)),
                  pl.BlockSpec((tm, B_WIDTH), lambda i, j: (i, 0)),
                  pl.BlockSpec((tm, C_WIDTH), lambda i, j: (i, 0)),
                  pl.BlockSpec((A_WIDTH, tn), lambda i, j: (0, j)),
                  pl.BlockSpec((B_WIDTH, tn), lambda i, j: (0, j)),
                  pl.BlockSpec((C_WIDTH, tn), lambda i, j: (0, j)),
                  pl.BlockSpec((tm, tn), lambda i, j: (i, j)),
                  pl.BlockSpec((tm, tn), lambda i, j: (i, nj + j)),
                  pl.BlockSpec((tm, tn), lambda i, j: (i, 2 * nj + j))],
        out_specs=pl.BlockSpec((tm, tn), lambda i, j: (i, j)),
        out_shape=jax.ShapeDtypeStruct((s, d), BF16),
        compiler_params=_cparams(("parallel", "arbitrary"), est),
        name="branch_merge",
    )(ya, yb, yc, wa, wb, wc, gates, gates, gates)


def _glu_kernel(h_ref, wg_ref, wu_ref, o_ref):
    h = h_ref[...]
    g = jnp.dot(h, wg_ref[...], preferred_element_type=F32)
    u = jnp.dot(h, wu_ref[...], preferred_element_type=F32)
    o_ref[...] = (g * _sigmoid(g) * u).astype(o_ref.dtype)


def _glu(h, wg, wu):
    s, k = h.shape
    f = wg.shape[1]
    tm, tf = TM_GLU, TF_GLU
    est = (2 * (_nbytes((tm, k), BF16) + 2 * _nbytes((k, tf), BF16) + _nbytes((tm, tf), BF16))
           + 3 * _nbytes((tm, tf), F32))
    return pl.pallas_call(
        _glu_kernel,
        grid=(s // tm, f // tf),
        in_specs=[pl.BlockSpec((tm, k), lambda i, j: (i, 0)),
                  pl.BlockSpec((k, tf), lambda i, j: (0, j)),
                  pl.BlockSpec((k, tf), lambda i, j: (0, j))],
        out_specs=pl.BlockSpec((tm, tf), lambda i, j: (i, j)),
        out_shape=jax.ShapeDtypeStruct((s, f), BF16),
        compiler_params=_cparams(("parallel", "arbitrary"), est),
        name="ffn_glu",
    )(h, wg, wu)


def _rope_tables(s):
    half = HEAD_DIM // 2
    pos = jnp.arange(s, dtype=jnp.int32)
    row = (pos // GRID_W).astype(F32)
    col = (pos % GRID_W).astype(F32)
    inv = ROPE_THETA ** (-jnp.arange(0, half, 2, dtype=F32) / half)
    ang_r = row[:, None] * inv[None, :]
    ang_c = col[:, None] * inv[None, :]
    ang = jnp.concatenate([ang_r, ang_r, ang_c, ang_c], axis=-1)
    sign = jnp.tile(jnp.concatenate([-jnp.ones((half // 2,), F32), jnp.ones((half // 2,), F32)]), 2)
    return jnp.cos(ang), jnp.sin(ang) * sign[None, :]


def _qk_prep_kernel(z_ref, g_ref, cos_ref, sin_ref, o_ref):
    y = _rmsnorm_rows(z_ref[...].astype(F32), g_ref[...])
    lane = lax.broadcasted_iota(jnp.int32, y.shape, 1)
    quarter = HEAD_DIM // 4
    first = (lane % (2 * quarter)) < quarter
    partner = jnp.where(first, pltpu.roll(y, HEAD_DIM - quarter, 1), pltpu.roll(y, quarter, 1))
    o_ref[...] = (y * cos_ref[...] + partner * sin_ref[...]).astype(o_ref.dtype)


def _qk_prep(z3, gains, cos, sin_signed):
    s = z3.shape[1]
    ts = TS_PREP
    nsl = B_HEADS + B_KV_HEADS
    est = 2 * (2 * _nbytes((ts, LANE), BF16) + 2 * _nbytes((ts, LANE), F32)) + 6 * _nbytes((ts, LANE), F32)
    return pl.pallas_call(
        _qk_prep_kernel,
        grid=(s // ts, nsl),
        in_specs=[pl.BlockSpec((None, ts, LANE), lambda i, n: (SLAB_QB + n, i, 0)),
                  pl.BlockSpec((None, 1, LANE), lambda i, n: (n, 0, 0)),
                  pl.BlockSpec((ts, LANE), lambda i, n: (i, 0)),
                  pl.BlockSpec((ts, LANE), lambda i, n: (i, 0))],
        out_specs=pl.BlockSpec((None, ts, LANE), lambda i, n: (n, i, 0)),
        out_shape=jax.ShapeDtypeStruct((nsl, s, LANE), BF16),
        compiler_params=_cparams(("parallel", "arbitrary"), est),
        name="gqa_qk_prep",
    )(z3, gains, cos, sin_signed)


def _softmax_update_t(k_tiles, vt_tiles, q_chunk, m_ref, acc_ref, bias_chunk=None):
    n_chunks = m_ref.shape[1] // QCHUNK
    units = [(t, j) for t in range(len(k_tiles)) for j in range(n_chunks)]

    def scores(u):
        t, j = units[u]
        st = _qk_scores(k_tiles[t], q_chunk(j))
        return st if bias_chunk is None else st + bias_chunk(t, j)

    pending = [scores(u) for u in range(min(QK_AHEAD, len(units)))]
    for u, (t, j) in enumerate(units):
        cols = slice(j * QCHUNK, (j + 1) * QCHUNK)
        st = pending.pop(0)
        if u + QK_AHEAD < len(units):
            pending.append(scores(u + QK_AHEAD))
        m_prev = m_ref[:, cols]
        m_new = jnp.maximum(m_prev, jnp.max(st, axis=0, keepdims=True))
        alpha = jnp.exp2(m_prev - m_new)
        p = jnp.exp2(st - m_new).astype(BF16)
        acc_ref[:, cols] = alpha * acc_ref[:, cols] + jnp.dot(vt_tiles[t], p, preferred_element_type=F32)
        m_ref[:, cols] = m_new


def _softmax_init(m_ref, acc_ref):
    m_ref[...] = jnp.full(m_ref.shape, -jnp.inf, F32)
    acc_ref[...] = jnp.zeros(acc_ref.shape, F32)


def _softmax_result_t(acc_ref):
    acc = acc_ref[...]
    return acc[0:HEAD_DIM] / acc[HEAD_DIM:HEAD_DIM + 1]


def _transposed_v(v, tk):
    n, s, _ = v.shape
    vt = jnp.swapaxes(v.reshape(n, s // tk, tk, HEAD_DIM), 2, 3)
    ones = jnp.ones((n, s // tk, VT_ROWS - HEAD_DIM, tk), BF16)
    return jnp.concatenate([vt, ones], axis=2)


def _qk_scores(a, b):
    return lax.dot_general(a, b, (((1,), (1,)), ((), ())), preferred_element_type=F32)


def _gqa_kernel(q_ref, k_ref, vt_ref, o_ref, m_ref, acc_ref):
    tq = q_ref.shape[1]
    n_k, _, tk = vt_ref.shape
    per_head = tq // QCHUNK

    def q_chunk(j):
        r0 = (j % per_head) * QCHUNK
        return q_ref[j // per_head, r0:r0 + QCHUNK, :]

    _softmax_init(m_ref, acc_ref)

    def key_tiles(kg, carry):
        tiles = [kg * KV_UNROLL + t for t in range(KV_UNROLL)]
        _softmax_update_t([k_ref[pl.ds(pl.multiple_of(ki * tk, tk), tk), :] for ki in tiles],
                          [vt_ref[ki] for ki in tiles], q_chunk, m_ref, acc_ref)
        return carry

    lax.fori_loop(0, n_k // KV_UNROLL, key_tiles, 0)
    ot = _softmax_result_t(acc_ref)
    for g in range(B_GROUP):
        o_ref[:, g * HEAD_DIM:(g + 1) * HEAD_DIM] = ot[:, g * tq:(g + 1) * tq].T.astype(o_ref.dtype)


def _gqa(qk, vt):
    s = qk.shape[1]
    tq = TQ_GQA
    n_k, _, tk = vt.shape[1:]
    assert n_k % KV_UNROLL == 0
    cols = B_GROUP * tq
    est = (2 * (2 * _nbytes((cols, LANE), BF16) + _nbytes((s, LANE), BF16) + _nbytes((VT_ROWS, s), BF16))
           + 3 * _nbytes((VT_ROWS, cols), F32) + 2 * (QK_AHEAD + 2) * _nbytes((tk, QCHUNK), F32))
    return pl.pallas_call(
        _gqa_kernel,
        grid=(B_KV_HEADS, s // tq),
        in_specs=[pl.BlockSpec((B_GROUP, tq, LANE), lambda h, i: (h, i, 0)),
                  pl.BlockSpec((None, s, LANE), lambda h, i: (B_HEADS + h, 0, 0)),
                  pl.BlockSpec((None, n_k, VT_ROWS, tk), lambda h, i: (h, 0, 0, 0))],
        out_specs=pl.BlockSpec((tq, B_GROUP * HEAD_DIM), lambda h, i: (i, h)),
        out_shape=jax.ShapeDtypeStruct((s, B_WIDTH), BF16),
        scratch_shapes=[pltpu.VMEM((1, cols), F32), pltpu.VMEM((VT_ROWS, cols), F32)],
        compiler_params=_cparams(("parallel", "parallel"), est),
        name="gqa_attention",
    )(qk, qk, vt)


def _t5_bias_tile_t(t5_ref, h, key_offset, tk, tq):
    kpos = lax.broadcasted_iota(jnp.int32, (tk, tq), 0) + key_offset
    qpos = lax.broadcasted_iota(jnp.int32, (tk, tq), 1)
    rel = kpos - qpos
    nb = T5_BUCKETS // 2
    max_exact = nb // 2
    ret = jnp.where(rel > 0, nb, 0)
    n = jnp.abs(rel)
    nf = jnp.maximum(n, 1).astype(F32)
    large = max_exact + (jnp.log(nf / max_exact) / math.log(T5_MAX_DIST / max_exact)
                         * (nb - max_exact)).astype(jnp.int32)
    large = jnp.minimum(large, nb - 1)
    bucket = ret + jnp.where(n < max_exact, n, large)
    bias = jnp.full((tk, tq), t5_ref[h], F32)
    for b in range(1, T5_BUCKETS):
        bias = jnp.where(bucket == b, t5_ref[b * A_HEADS + h], bias)
    return bias * LOG2E


def _diff_kernel(t5_ref, lam_ref, q_ref, k_ref, vt_ref, g_ref, o_ref, qs_ref, bias_ref, m_ref, acc_ref):
    h, qi = pl.program_id(0), pl.program_id(1)
    tq = q_ref.shape[0]
    n_k, _, tk = vt_ref.shape
    n_sub = tq // tk
    n_near = n_sub + 2
    per_map = tq // QCHUNK

    @pl.when(qi == 0)
    def _():
        far_left = t5_ref[(T5_BUCKETS // 2 - 1) * A_HEADS + h] * LOG2E
        far_right = t5_ref[(T5_BUCKETS - 1) * A_HEADS + h] * LOG2E
        bias_ref[0] = jnp.full((tk, tq), far_left, F32)
        for o in range(n_near):
            bias_ref[o + 1] = _t5_bias_tile_t(t5_ref, h, (o - 1) * tk, tk, tq)
        bias_ref[n_near + 1] = jnp.full((tk, tq), far_right, F32)

    _softmax_init(m_ref, acc_ref)
    q = q_ref[...]
    lane = lax.broadcasted_iota(jnp.int32, q.shape, 1)
    zero = jnp.zeros_like(q)
    qs_ref[0:tq, :] = jnp.where(lane < A_QK_DIM, q, zero)
    qs_ref[tq:2 * tq, :] = jnp.where(lane >= A_QK_DIM, q, zero)

    def q_chunk(j):
        return qs_ref[j * QCHUNK:(j + 1) * QCHUNK, :]

    def key_tiles(kg, carry):
        tiles = [kg * KV_UNROLL + t for t in range(KV_UNROLL)]
        bias_idx = [jnp.clip(ki - n_sub * qi + 2, 0, n_near + 1) for ki in tiles]

        def bias_chunk(t, j):
            c0 = (j % per_map) * QCHUNK
            return bias_ref[bias_idx[t], :, c0:c0 + QCHUNK]

        _softmax_update_t([k_ref[pl.ds(pl.multiple_of(ki * tk, tk), tk), :] for ki in tiles],
                          [vt_ref[ki] for ki in tiles], q_chunk, m_ref, acc_ref, bias_chunk=bias_chunk)
        return carry

    lax.fori_loop(0, n_k // KV_UNROLL, key_tiles, 0)
    ot = _softmax_result_t(acc_ref)
    out = (ot[:, 0:tq] - lam_ref[0] * ot[:, tq:2 * tq]).T
    o_ref[...] = (_rmsnorm_rows(out, g_ref[...]) * lam_ref[1]).astype(o_ref.dtype)


def _diff_attention(z3, vt, t5_flat, lam_pair, subln_g):
    s = z3.shape[1]
    tq = TQ_DIFF
    n_k, _, tk = vt.shape[1:]
    assert tk >= T5_MAX_DIST and tq % tk == 0 and s % tq == 0 and tq % QCHUNK == 0 and n_k % KV_UNROLL == 0
    n_near = tq // tk + 2
    est = (2 * (2 * _nbytes((tq, LANE), BF16) + _nbytes((s, LANE), BF16) + _nbytes((VT_ROWS, s), BF16))
           + _nbytes((2 * tq, LANE), BF16) + (n_near + 2) * _nbytes((tk, tq), F32)
           + 3 * _nbytes((VT_ROWS, 2 * tq), F32) + 2 * (QK_AHEAD + 2) * _nbytes((tk, QCHUNK), F32))
    grid_spec = pltpu.PrefetchScalarGridSpec(
        num_scalar_prefetch=2,
        grid=(A_HEADS, s // tq),
        in_specs=[pl.BlockSpec((None, tq, LANE), lambda h, i, *_: (SLAB_QA + h, i, 0)),
                  pl.BlockSpec((None, s, LANE), lambda h, i, *_: (SLAB_KA + h, 0, 0)),
                  pl.BlockSpec((None, n_k, VT_ROWS, tk), lambda h, i, *_: (h, 0, 0, 0)),
                  pl.BlockSpec((1, LANE), lambda h, i, *_: (0, 0))],
        out_specs=pl.BlockSpec((tq, HEAD_DIM), lambda h, i, *_: (i, h)),
        scratch_shapes=[pltpu.VMEM((2 * tq, LANE), BF16), pltpu.VMEM((n_near + 2, tk, tq), F32),
                        pltpu.VMEM((1, 2 * tq), F32), pltpu.VMEM((VT_ROWS, 2 * tq), F32)])
    return pl.pallas_call(
        _diff_kernel,
        grid_spec=grid_spec,
        out_shape=jax.ShapeDtypeStruct((s, A_WIDTH), BF16),
        compiler_params=_cparams(("arbitrary", "arbitrary"), est),
        name="diff_attention",
    )(t5_flat, lam_pair, z3, z3, vt, subln_g.reshape(1, LANE))


def _na_window_start(first_row, rows):
    return max(0, min(first_row - NA_ROWS // 2, rows - NA_WIN_ROWS))


def _na_build_bias(rpb_ref, h, bias_ref, rows):
    cq = lax.broadcasted_iota(jnp.int32, (GRID_W, LANE), 0)
    ck = lax.broadcasted_iota(jnp.int32, (GRID_W, LANE), 1) % GRID_W
    cs = jnp.clip(cq - NA_COLS // 2, 0, GRID_W - NA_COLS)
    col_valid = (ck >= cs) & (ck < cs + NA_COLS)
    coff = jnp.clip(ck - cq + (NA_COLS - 1), 0, NA_COFF - 1)
    lower_half = lax.broadcasted_iota(jnp.int32, (GRID_W, LANE), 1) < GRID_W
    masked = jnp.full((GRID_W, LANE), MASK_VALUE, F32)
    col_bias = []
    for ro in range(NA_ROFF):
        base = (h * NA_ROFF + ro) * NA_COFF
        b = jnp.full((GRID_W, LANE), rpb_ref[base], F32)
        for c in range(1, NA_COFF):
            b = jnp.where(coff == c, rpb_ref[base + c], b)
        col_bias.append(jnp.where(col_valid, b * LOG2E, MASK_VALUE))
    n_groups = rows // NA_GROUP_ROWS
    for var, group in enumerate((0, 1, n_groups - 1)):
        ws = _na_window_start(group * NA_GROUP_ROWS, rows)
        for a in range(NA_GROUP_ROWS):
            r = group * NA_GROUP_ROWS + a
            rs = max(0, min(r - NA_ROWS // 2, rows - NA_ROWS))
            for ip in range(NA_WIN_ROWS // 2):
                halves = []
                for kr in (ws + 2 * ip, ws + 2 * ip + 1):
                    valid = rs <= kr < rs + NA_ROWS
                    halves.append(col_bias[kr - r + NA_ROWS - 1] if valid else masked)
                bias_ref[var, a * GRID_W:(a + 1) * GRID_W, ip * LANE:(ip + 1) * LANE] = (
                    jnp.where(lower_half, halves[0], halves[1]))


def _na_kernel(rpb_ref, q_ref, k_ref, v_ref, o_ref, bias_ref, *, rows):
    h, g = pl.program_id(0), pl.program_id(1)
    n_groups = rows // NA_GROUP_ROWS
    win = NA_WIN_ROWS * GRID_W

    @pl.when(g == 0)
    def _():
        _na_build_bias(rpb_ref, h, bias_ref, rows)

    ws = jnp.clip(g * NA_GROUP_ROWS - NA_ROWS // 2, 0, rows - NA_WIN_ROWS)
    start = pl.multiple_of(ws * GRID_W, (NA_ROWS // 2) * GRID_W)
    kw = k_ref[pl.ds(start, win), :]
    vw = v_ref[pl.ds(start, win), :]
    var = jnp.where(g == 0, 0, jnp.where(g == n_groups - 1, 2, 1))
    s = _qk_scores(q_ref[...], kw) + bias_ref[var]
    m = jnp.max(s, axis=-1, keepdims=True)
    p = jnp.exp2(s - m)
    l = jnp.sum(p, axis=-1, keepdims=True)
    o = jnp.dot(p.astype(BF16), vw, preferred_element_type=F32) / l
    o_ref[...] = o.astype(o_ref.dtype)


def _neighbourhood_attention(z3, rpb_flat):
    s = z3.shape[1]
    rows = s // GRID_W
    assert rows % NA_GROUP_ROWS == 0 and rows >= 3 * NA_GROUP_ROWS
    tq = NA_GROUP_ROWS * GRID_W
    win = NA_WIN_ROWS * GRID_W
    est = (2 * (2 * _nbytes((tq, LANE), BF16) + 2 * _nbytes((s, LANE), BF16))
           + 3 * _nbytes((tq, win), F32) + 3 * _nbytes((tq, win), F32))
    grid_spec = pltpu.PrefetchScalarGridSpec(
        num_scalar_prefetch=1,
        grid=(C_HEADS, rows // NA_GROUP_ROWS),
        in_specs=[pl.BlockSpec((None, tq, LANE), lambda h, g, *_: (SLAB_QC + h, g, 0)),
                  pl.BlockSpec((None, s, LANE), lambda h, g, *_: (SLAB_KC + h, 0, 0)),
                  pl.BlockSpec((None, s, LANE), lambda h, g, *_: (SLAB_VC + h, 0, 0))],
        out_specs=pl.BlockSpec((tq, HEAD_DIM), lambda h, g, *_: (g, h)),
        scratch_shapes=[pltpu.VMEM((3, tq, win), F32)])
    return pl.pallas_call(
        functools.partial(_na_kernel, rows=rows),
        grid_spec=grid_spec,
        out_shape=jax.ShapeDtypeStruct((s, C_WIDTH), BF16),
        compiler_params=_cparams(("arbitrary", "arbitrary"), est),
        name="neighbourhood_attention",
    )(rpb_flat, z3, z3, z3)


def _router_kernel(x_ref, g_ref, wr_ref, o_ref):
    h = _rmsnorm_rows(x_ref[...], g_ref[...])
    logits = jnp.dot(h, wr_ref[...], preferred_element_type=F32, precision=lax.Precision.HIGHEST)
    lane = lax.broadcasted_iota(jnp.int32, logits.shape, 1)
    lane_f = lane.astype(F32)
    logits = jnp.where(lane < N_EXPERTS, logits, -jnp.inf)
    m1 = jnp.max(logits, axis=-1, keepdims=True)
    i1 = jnp.min(jnp.where(logits == m1, lane_f, float(LANE)), axis=-1, keepdims=True)
    rest = jnp.where(lane_f == i1, -jnp.inf, logits)
    m2 = jnp.max(rest, axis=-1, keepdims=True)
    i2 = jnp.min(jnp.where(rest == m2, lane_f, float(LANE)), axis=-1, keepdims=True)
    e = jnp.exp(m2 - m1)
    g1 = 1.0 / (1.0 + e)
    g2 = e / (1.0 + e)
    o_ref[...] = jnp.where(lane == 0, i1, jnp.where(lane == 1, i2, jnp.where(lane == 2, g1, g2)))


def _router(x, g, w_router):
    s, d = x.shape
    tm = TM_ROUTE
    wr = jnp.zeros((d, LANE), F32).at[:, :N_EXPERTS].set(w_router)
    est = 2 * (_nbytes((tm, d), F32) + _nbytes((d, LANE), F32) + _nbytes((tm, LANE), F32)) + 4 * _nbytes((tm, d), F32)
    return pl.pallas_call(
        _router_kernel,
        grid=(s // tm,),
        in_specs=[pl.BlockSpec((tm, d), lambda i: (i, 0)),
                  pl.BlockSpec((1, d), lambda i: (0, 0)),
                  pl.BlockSpec((d, LANE), lambda i: (0, 0))],
        out_specs=pl.BlockSpec((tm, LANE), lambda i: (i, 0)),
        out_shape=jax.ShapeDtypeStruct((s, LANE), F32),
        compiler_params=_cparams(("parallel",), est),
        name="moe_router",
    )(x, g.reshape(1, d), wr)


def _row_gather(src_hbm, dst_ref, sem, n_rows, row_of):
    def issue(r, carry):
        pltpu.make_async_copy(src_hbm.at[pl.ds(row_of(r), 1)], dst_ref.at[pl.ds(r, 1)], sem).start()
        return carry
    lax.fori_loop(0, n_rows, issue, 0)
    pltpu.make_async_copy(src_hbm.at[pl.ds(0, n_rows)], dst_ref.at[pl.ds(0, n_rows)], sem).wait()


def _moe_kernel(tile_e_ref, nused_ref, tok_ref, x_hbm, gn_ref, wg_ref, wu_ref, wd_ref,
                o_ref, xg_ref, xb_ref, acc_ref, sem):
    i, f = pl.program_id(0), pl.program_id(1)
    tm = xg_ref.shape[0]
    used = i < nused_ref[0]
    last = f == pl.num_programs(1) - 1

    @pl.when(used & (f == 0))
    def _():
        _row_gather(x_hbm, xg_ref, sem, tm, lambda r: tok_ref[i * tm + r])
        xb_ref[...] = _rmsnorm_rows(xg_ref[...], gn_ref[...]).astype(BF16)
        acc_ref[...] = jnp.zeros(acc_ref.shape, F32)

    @pl.when(used)
    def _():
        hb = xb_ref[...]
        g = jnp.dot(hb, wg_ref[...], preferred_element_type=F32)
        u = jnp.dot(hb, wu_ref[...], preferred_element_type=F32)
        act = (g * _sigmoid(g) * u).astype(BF16)
        acc_ref[...] += jnp.dot(act, wd_ref[...], preferred_element_type=F32)

    @pl.when(used & last)
    def _():
        o_ref[...] = acc_ref[...]

    @pl.when(jnp.logical_not(used) & last)
    def _():
        o_ref[...] = jnp.zeros(o_ref.shape, F32)


def _moe_experts(x, gn, wg, wu, wd, tile_e, n_used, buf_tok):
    d = x.shape[1]
    fdim = wg.shape[2]
    cap = buf_tok.shape[0]
    tm, tf = TM_MOE, TF_MOE
    nf = fdim // tf

    def fsel(i, f, nused):
        return jnp.where(i < nused[0], f, nf - 1)

    est = (2 * (3 * _nbytes((d, tf), BF16) + _nbytes((tm, d), F32))
           + 2 * _nbytes((tm, d), F32) + _nbytes((tm, d), BF16)
           + 3 * _nbytes((tm, tf), F32) + _nbytes((tm, d), F32))
    grid_spec = pltpu.PrefetchScalarGridSpec(
        num_scalar_prefetch=3,
        grid=(cap // tm, nf),
        in_specs=[pl.BlockSpec(memory_space=pl.ANY),
                  pl.BlockSpec((1, d), lambda i, f, te, nu, tok: (0, 0)),
                  pl.BlockSpec((None, d, tf), lambda i, f, te, nu, tok: (te[i], 0, fsel(i, f, nu))),
                  pl.BlockSpec((None, d, tf), lambda i, f, te, nu, tok: (te[i], 0, fsel(i, f, nu))),
                  pl.BlockSpec((None, tf, d), lambda i, f, te, nu, tok: (te[i], fsel(i, f, nu), 0))],
        out_specs=pl.BlockSpec((tm, d), lambda i, f, te, nu, tok: (i, 0)),
        scratch_shapes=[pltpu.VMEM((tm, d), F32), pltpu.VMEM((tm, d), BF16),
                        pltpu.VMEM((tm, d), F32), pltpu.SemaphoreType.DMA(())])
    return pl.pallas_call(
        _moe_kernel,
        grid_spec=grid_spec,
        out_shape=jax.ShapeDtypeStruct((cap, d), F32),
        compiler_params=_cparams(("arbitrary", "arbitrary"), est),
        name="moe_experts",
    )(tile_e, n_used, buf_tok, x, gn.reshape(1, d), wg, wu, wd)


def _combine_kernel(pos_ref, x_ref, route_ref, y_hbm, gf_ref, o_ref, buf_ref, sem, *, final_norm):
    i = pl.program_id(0)
    tm = x_ref.shape[0]
    slot = i % 2

    def start_gather(step, dst_slot):
        def issue(r, carry):
            row = pos_ref[TOP_K * (step * tm + r % tm) + r // tm]
            pltpu.make_async_copy(y_hbm.at[pl.ds(row, 1)], buf_ref.at[dst_slot, pl.ds(r, 1)],
                                  sem.at[dst_slot]).start()
            return carry
        lax.fori_loop(0, TOP_K * tm, issue, 0)

    @pl.when(i == 0)
    def _():
        start_gather(0, 0)

    @pl.when(i + 1 < pl.num_programs(0))
    def _():
        start_gather(i + 1, 1 - slot)

    pltpu.make_async_copy(y_hbm.at[pl.ds(0, TOP_K * tm)], buf_ref.at[slot], sem.at[slot]).wait()
    route = route_ref[...]
    y = (route[:, TOP_K:TOP_K + 1] * buf_ref[slot, 0:tm, :]
         + route[:, TOP_K + 1:TOP_K + 2] * buf_ref[slot, tm:2 * tm, :])
    out = x_ref[...] + y
    if final_norm:
        out = _rmsnorm_rows(out, gf_ref[...])
    o_ref[...] = out


def _moe_combine(x, yb, pos_flat, route, final_gain):
    s, d = x.shape
    tm = TM_COMB
    final_norm = final_gain is not None
    gf = final_gain if final_norm else jnp.ones((d,), F32)
    est = 2 * (2 * _nbytes((tm, d), F32) + _nbytes((tm, LANE), F32)) + 2 * _nbytes((TOP_K * tm, d), F32) \
        + 3 * _nbytes((tm, d), F32)
    grid_spec = pltpu.PrefetchScalarGridSpec(
        num_scalar_prefetch=1,
        grid=(s // tm,),
        in_specs=[pl.BlockSpec((tm, d), lambda i, pos: (i, 0)),
                  pl.BlockSpec((tm, LANE), lambda i, pos: (i, 0)),
                  pl.BlockSpec(memory_space=pl.ANY),
                  pl.BlockSpec((1, d), lambda i, pos: (0, 0))],
        out_specs=pl.BlockSpec((tm, d), lambda i, pos: (i, 0)),
        scratch_shapes=[pltpu.VMEM((2, TOP_K * tm, d), F32), pltpu.SemaphoreType.DMA((2,))])
    return pl.pallas_call(
        functools.partial(_combine_kernel, final_norm=final_norm),
        grid_spec=grid_spec,
        out_shape=jax.ShapeDtypeStruct((s, d), F32),
        compiler_params=_cparams(("arbitrary",), est),
        name="moe_combine",
    )(pos_flat, x, route, yb, gf.reshape(1, d))


def _moe_layer(x, gn, w_router, wg, wu, wd, final_gain=None):
    s = x.shape[0]
    tm = TM_MOE
    route = _router(x, gn, w_router)
    e_flat = route[:, 0:TOP_K].astype(jnp.int32).reshape(-1)
    n_assign = s * TOP_K
    onehot = (e_flat[:, None] == jnp.arange(N_EXPERTS, dtype=jnp.int32)[None, :]).astype(jnp.int32)
    csum = jnp.cumsum(onehot, axis=0)
    rank = jnp.sum(csum * onehot, axis=1) - 1
    counts = csum[-1]
    pcounts = (counts + tm - 1) // tm * tm
    pends = jnp.cumsum(pcounts)
    pstarts = pends - pcounts
    dest = jnp.sum(pstarts[None, :] * onehot, axis=1) + rank
    cap = n_assign + N_EXPERTS * tm
    tok_flat = jnp.arange(n_assign, dtype=jnp.int32) // TOP_K
    buf_tok = jnp.zeros((cap,), jnp.int32).at[dest].set(tok_flat)
    n_tiles = cap // tm
    tile_e = jnp.minimum(jnp.searchsorted(pends, jnp.arange(n_tiles, dtype=jnp.int32) * tm, side='right'),
                         N_EXPERTS - 1).astype(jnp.int32)
    n_used = (pends[-1:] // tm).astype(jnp.int32)
    yb = _moe_experts(x, gn, wg, wu, wd, tile_e, n_used, buf_tok)
    return _moe_combine(x, yb, dest.astype(jnp.int32), route, final_gain)


def kernel(x, w_in, w_branch_a, w_branch_b, w_branch_c, w_out, norm_mix, norm_ffn, norm_final, t5_bias, diff_lambda, diff_subln, qk_norm_b, na_rpb, ffn_gate, ffn_up, ffn_down, moe_router, moe_gate, moe_up, moe_down):
    bn, s, d = x.shape
    assert bn == 1 and d == D_MODEL
    depth = w_in.shape[0]
    xs = x.reshape(s, d)
    cos, sin_signed = _rope_tables(s)
    t5_flat = t5_bias.astype(F32).reshape(-1)
    colscale = jnp.ones((1, QKV_WIDTH), F32)
    colscale = colscale.at[:, SLAB_QA * LANE:SLAB_KA * LANE].set(A_QK_DIM ** -0.5 * LOG2E)
    colscale = colscale.at[:, SLAB_QC * LANE:SLAB_KC * LANE].set(HEAD_DIM ** -0.5 * LOG2E)

    for l in range(depth):
        h = _rmsnorm(xs, norm_mix[l], BF16)
        w_l = w_in[l].astype(BF16)
        z3 = _proj_slabs(h, w_l, colscale)
        gates = _matmul_sigmoid(h, w_l, QKV_WIDTH)

        lam_init = 0.8 - 0.6 * math.exp(-0.3 * l)
        lp = diff_lambda[l].astype(F32)
        lam = jnp.exp(jnp.sum(lp[0] * lp[1])) - jnp.exp(jnp.sum(lp[2] * lp[3])) + lam_init
        lam_pair = jnp.stack([lam, jnp.asarray(1.0 - lam_init, F32)]).astype(F32)
        vt_a = _transposed_v(z3[SLAB_VA:SLAB_VA + A_HEADS], TK_DIFF)
        vt_b = _transposed_v(z3[SLAB_VB:SLAB_VB + B_KV_HEADS], TK_GQA)
        ya = _diff_attention(z3, vt_a, t5_flat, lam_pair, diff_subln[l].astype(F32))

        q_gain = qk_norm_b[l, 0].astype(F32) * (HEAD_DIM ** -0.5 * LOG2E)
        k_gain = qk_norm_b[l, 1].astype(F32)
        gains = jnp.concatenate([jnp.tile(q_gain[None], (B_HEADS, 1)),
                                 jnp.tile(k_gain[None], (B_KV_HEADS, 1))])[:, None, :]
        yb = _gqa(_qk_prep(z3, gains, cos, sin_signed), vt_b)

        yc = _neighbourhood_attention(z3, na_rpb[l].astype(F32).reshape(-1))

        merged = _merge(ya, yb, yc, w_branch_a[l].astype(BF16), w_branch_b[l].astype(BF16),
                        w_branch_c[l].astype(BF16), gates)
        xs = _matmul_residual(merged, w_out[l].astype(BF16), xs, TM_OUT, TN_OUT, "out_proj")

        j = l // 2
        if l % 2 == 0:
            hf = _rmsnorm(xs, norm_ffn[l], BF16)
            act = _glu(hf, ffn_gate[j].astype(BF16), ffn_up[j].astype(BF16))
            xs = _matmul_residual(act, ffn_down[j].astype(BF16), xs, TM_DOWN, TN_DOWN, "ffn_down")
        else:
            last = l == depth - 1
            xs = _moe_layer(xs, norm_ffn[l], moe_router[j], moe_gate[j].astype(BF16),
                            moe_up[j].astype(BF16), moe_down[j].astype(BF16),
                            final_gain=norm_final.astype(F32) if last else None)
            if last:
                return xs.astype(x.dtype).reshape(bn, s, d)
    return _rmsnorm(xs, norm_final, x.dtype).reshape(bn, s, d)
```

```python
import functools
import math

import jax
import jax.numpy as jnp
from jax import lax
from jax.experimental import pallas as pl
from jax.experimental.pallas import tpu as pltpu

F32 = jnp.float32
BF16 = jnp.bfloat16

D_MODEL = 2048
GRID_W = 64
HEAD_DIM = 128
NORM_EPS = 1e-6
A_HEADS = 4
A_QK_DIM = 64
A_WIDTH = A_HEADS * HEAD_DIM
B_HEADS = 8
B_KV_HEADS = 2
B_GROUP = B_HEADS // B_KV_HEADS
B_WIDTH = B_HEADS * HEAD_DIM
C_HEADS = 4
C_WIDTH = C_HEADS * HEAD_DIM
NA_ROWS = 8
NA_COLS = 16
NA_ROFF = 2 * NA_ROWS - 1
NA_COFF = 2 * NA_COLS - 1
T5_BUCKETS = 32
T5_MAX_DIST = 128
ROPE_THETA = 10000.0
N_EXPERTS = 8
TOP_K = 2
QKV_WIDTH = 2 * A_WIDTH + A_WIDTH + B_WIDTH + 2 * B_KV_HEADS * HEAD_DIM + 3 * C_WIDTH
LOG2E = 1.4426950408889634
MASK_VALUE = -1e30

LANE = 128
V7X_VMEM_BYTES = 64 * 1024 * 1024
VMEM_CAP_BYTES = 56 * 1024 * 1024
QCHUNK = 256

SLAB_QA, SLAB_KA, SLAB_VA = 0, 4, 8
SLAB_QB, SLAB_KB, SLAB_VB = 12, 20, 22
SLAB_QC, SLAB_KC, SLAB_VC = 24, 28, 32
N_SLABS = QKV_WIDTH // LANE
VT_ROWS = HEAD_DIM + 16

TM_NORM = 512
TM_PROJ, TN_PROJ = 1024, 1536
TS_PREP = 1024
TQ_DIFF, TK_DIFF = 1024, 512
TQ_GQA, TK_GQA = 512, 512
KV_UNROLL = 4
QK_AHEAD = 3
NA_GROUP_ROWS = 8
NA_WIN_ROWS = 16
TM_MERGE, TN_MERGE = 512, 1024
TM_OUT, TN_OUT = 1024, 1024
TM_GLU, TF_GLU = 1024, 512
TM_DOWN, TN_DOWN = 512, 512
TM_ROUTE = 512
TM_MOE, TF_MOE = 512, 1024
TM_COMB = 256
DMA_ISSUE_UNROLL = 8


def _cparams(semantics, vmem_bytes):
    return pltpu.CompilerParams(dimension_semantics=semantics,
                                vmem_limit_bytes=int(min(vmem_bytes, VMEM_CAP_BYTES)))


def _nbytes(shape, dtype):
    return math.prod(shape) * jnp.dtype(dtype).itemsize


def _rmsnorm_rows(x, g):
    ms = jnp.mean(x * x, axis=-1, keepdims=True)
    return x * lax.rsqrt(ms + NORM_EPS) * g


def _rmsnorm_kernel(x_ref, g_ref, o_ref):
    o_ref[...] = _rmsnorm_rows(x_ref[...], g_ref[...]).astype(o_ref.dtype)


def _rmsnorm(x, g, out_dtype):
    s, d = x.shape
    tm = TM_NORM
    est = 2 * (_nbytes((tm, d), F32) + _nbytes((tm, d), out_dtype)) + 3 * _nbytes((tm, d), F32)
    return pl.pallas_call(
        _rmsnorm_kernel,
        grid=(s // tm,),
        in_specs=[pl.BlockSpec((tm, d), lambda i: (i, 0)),
                  pl.BlockSpec((1, d), lambda i: (0, 0))],
        out_specs=pl.BlockSpec((tm, d), lambda i: (i, 0)),
        out_shape=jax.ShapeDtypeStruct((s, d), out_dtype),
        compiler_params=_cparams(("parallel",), est),
        name="rmsnorm",
    )(x, g.reshape(1, d))


def _proj_slab_kernel(a_ref, w_ref, cs_ref, o_ref):
    acc = jnp.dot(a_ref[...], w_ref[...], preferred_element_type=F32) * cs_ref[...]
    for c in range(o_ref.shape[0]):
        o_ref[c] = acc[:, c * LANE:(c + 1) * LANE].astype(o_ref.dtype)


def _proj_slabs(a, w, colscale):
    s, k = a.shape
    n = colscale.shape[1]
    tm, tn = TM_PROJ, TN_PROJ
    spt = tn // LANE
    est = (2 * (_nbytes((tm, k), BF16) + _nbytes((k, tn), BF16) + _nbytes((tm, tn), BF16))
           + 2 * _nbytes((tm, tn), F32))
    return pl.pallas_call(
        _proj_slab_kernel,
        grid=(s // tm, n // tn),
        in_specs=[pl.BlockSpec((tm, k), lambda i, j: (i, 0)),
                  pl.BlockSpec((k, tn), lambda i, j: (0, j)),
                  pl.BlockSpec((1, tn), lambda i, j: (0, j))],
        out_specs=pl.BlockSpec((spt, tm, LANE), lambda i, j: (j, i, 0)),
        out_shape=jax.ShapeDtypeStruct((n // LANE, s, LANE), BF16),
        compiler_params=_cparams(("parallel", "arbitrary"), est),
        name="qkv_proj",
    )(a, w, colscale)


def _sigmoid(x):
    return 1.0 / (1.0 + jnp.exp(-x))


def _matmul_sigmoid_kernel(a_ref, w_ref, o_ref):
    acc = jnp.dot(a_ref[...], w_ref[...], preferred_element_type=F32)
    o_ref[...] = _sigmoid(acc).astype(o_ref.dtype)


def _matmul_sigmoid(a, w, col0):
    s, k = a.shape
    tm, tn = TM_PROJ, TN_PROJ
    n = w.shape[1] - col0
    assert col0 % tn == 0 and n % tn == 0
    j0 = col0 // tn
    est = (2 * (_nbytes((tm, k), BF16) + _nbytes((k, tn), BF16) + _nbytes((tm, tn), BF16))
           + 2 * _nbytes((tm, tn), F32))
    return pl.pallas_call(
        _matmul_sigmoid_kernel,
        grid=(s // tm, n // tn),
        in_specs=[pl.BlockSpec((tm, k), lambda i, j: (i, 0)),
                  pl.BlockSpec((k, tn), lambda i, j: (0, j0 + j))],
        out_specs=pl.BlockSpec((tm, tn), lambda i, j: (i, j)),
        out_shape=jax.ShapeDtypeStruct((s, n), BF16),
        compiler_params=_cparams(("parallel", "arbitrary"), est),
        name="gate_proj",
    )(a, w)


def _matmul_residual_kernel(a_ref, w_ref, r_ref, o_ref):
    o_ref[...] = r_ref[...] + jnp.dot(a_ref[...], w_ref[...], preferred_element_type=F32)


def _matmul_residual(a, w, r, tm, tn, name):
    s, k = a.shape
    n = w.shape[1]
    est = (2 * (_nbytes((tm, k), BF16) + _nbytes((k, tn), BF16) + 2 * _nbytes((tm, tn), F32))
           + _nbytes((tm, tn), F32))
    return pl.pallas_call(
        _matmul_residual_kernel,
        grid=(s // tm, n // tn),
        in_specs=[pl.BlockSpec((tm, k), lambda i, j: (i, 0)),
                  pl.BlockSpec((k, tn), lambda i, j: (0, j)),
                  pl.BlockSpec((tm, tn), lambda i, j: (i, j))],
        out_specs=pl.BlockSpec((tm, tn), lambda i, j: (i, j)),
        out_shape=jax.ShapeDtypeStruct((s, n), F32),
        compiler_params=_cparams(("parallel", "arbitrary"), est),
        name=name,
    )(a, w, r)


def _merge_kernel(ya_ref, yb_ref, yc_ref, wa_ref, wb_ref, wc_ref, ga_ref, gb_ref, gc_ref, o_ref):
    acc = ga_ref[...].astype(F32) * jnp.dot(ya_ref[...], wa_ref[...], preferred_element_type=F32)
    acc += gb_ref[...].astype(F32) * jnp.dot(yb_ref[...], wb_ref[...], preferred_element_type=F32)
    acc += gc_ref[...].astype(F32) * jnp.dot(yc_ref[...], wc_ref[...], preferred_element_type=F32)
    o_ref[...] = acc.astype(o_ref.dtype)


def _merge(ya, yb, yc, wa, wb, wc, gates):
    s = ya.shape[0]
    d = wa.shape[1]
    tm, tn = TM_MERGE, TN_MERGE
    nj = d // tn
    kin = A_WIDTH + B_WIDTH + C_WIDTH
    est = (2 * (_nbytes((tm, kin), BF16) + _nbytes((kin, tn), BF16) + 4 * _nbytes((tm, tn), BF16))
           + 4 * _nbytes((tm, tn), F32))
    return pl.pallas_call(
        _merge_kernel,
        grid=(s // tm, nj),
        in_specs=[pl.BlockSpec((tm, A_WIDTH), lambda i, j: (i, 0)),
                  pl.BlockSpec((tm, B_WIDTH), lambda i, j: (i, 0)),
                  pl.BlockSpec((tm, C_WIDTH), lambda i, j: (i, 0)),
                  pl.BlockSpec((A_WIDTH, tn), lambda i, j: (0, j)),
                  pl.BlockSpec((B_WIDTH, tn), lambda i, j: (0, j)),
                  pl.BlockSpec((C_WIDTH, tn), lambda i, j: (0, j)),
                  pl.BlockSpec((tm, tn), lambda i, j: (i, j)),
                  pl.BlockSpec((tm, tn), lambda i, j: (i, nj + j)),
                  pl.BlockSpec((tm, tn), lambda i, j: (i, 2 * nj + j))],
        out_specs=pl.BlockSpec((tm, tn), lambda i, j: (i, j)),
        out_shape=jax.ShapeDtypeStruct((s, d), BF16),
        compiler_params=_cparams(("parallel", "arbitrary"), est),
        name="branch_merge",
    )(ya, yb, yc, wa, wb, wc, gates, gates, gates)


def _glu_kernel(h_ref, wg_ref, wu_ref, o_ref):
    h = h_ref[...]
    g = jnp.dot(h, wg_ref[...], preferred_element_type=F32)
    u = jnp.dot(h, wu_ref[...], preferred_element_type=F32)
    o_ref[...] = (g * _sigmoid(g) * u).astype(o_ref.dtype)


def _glu(h, wg, wu):
    s, k = h.shape
    f = wg.shape[1]
    tm, tf = TM_GLU, TF_GLU
    est = (2 * (_nbytes((tm, k), BF16) + 2 * _nbytes((k, tf), BF16) + _nbytes((tm, tf), BF16))
           + 3 * _nbytes((tm, tf), F32))
    return pl.pallas_call(
        _glu_kernel,
        grid=(s // tm, f // tf),
        in_specs=[pl.BlockSpec((tm, k), lambda i, j: (i, 0)),
                  pl.BlockSpec((k, tf), lambda i, j: (0, j)),
                  pl.BlockSpec((k, tf), lambda i, j: (0, j))],
        out_specs=pl.BlockSpec((tm, tf), lambda i, j: (i, j)),
        out_shape=jax.ShapeDtypeStruct((s, f), BF16),
        compiler_params=_cparams(("parallel", "arbitrary"), est),
        name="ffn_glu",
    )(h, wg, wu)


def _rope_tables(s):
    half = HEAD_DIM // 2
    pos = jnp.arange(s, dtype=jnp.int32)
    row = (pos // GRID_W).astype(F32)
    col = (pos % GRID_W).astype(F32)
    inv = ROPE_THETA ** (-jnp.arange(0, half, 2, dtype=F32) / half)
    ang_r = row[:, None] * inv[None, :]
    ang_c = col[:, None] * inv[None, :]
    ang = jnp.concatenate([ang_r, ang_r, ang_c, ang_c], axis=-1)
    sign = jnp.tile(jnp.concatenate([-jnp.ones((half // 2,), F32), jnp.ones((half // 2,), F32)]), 2)
    return jnp.cos(ang), jnp.sin(ang) * sign[None, :]


def _qk_prep_kernel(z_ref, g_ref, cos_ref, sin_ref, o_ref):
    y = _rmsnorm_rows(z_ref[...].astype(F32), g_ref[...])
    lane = lax.broadcasted_iota(jnp.int32, y.shape, 1)
    quarter = HEAD_DIM // 4
    first = (lane % (2 * quarter)) < quarter
    partner = jnp.where(first, pltpu.roll(y, HEAD_DIM - quarter, 1), pltpu.roll(y, quarter, 1))
    o_ref[...] = (y * cos_ref[...] + partner * sin_ref[...]).astype(o_ref.dtype)


def _qk_prep(z3, gains, cos, sin_signed):
    s = z3.shape[1]
    ts = TS_PREP
    nsl = B_HEADS + B_KV_HEADS
    est = 2 * (2 * _nbytes((ts, LANE), BF16) + 2 * _nbytes((ts, LANE), F32)) + 6 * _nbytes((ts, LANE), F32)
    return pl.pallas_call(
        _qk_prep_kernel,
        grid=(s // ts, nsl),
        in_specs=[pl.BlockSpec((None, ts, LANE), lambda i, n: (SLAB_QB + n, i, 0)),
                  pl.BlockSpec((None, 1, LANE), lambda i, n: (n, 0, 0)),
                  pl.BlockSpec((ts, LANE), lambda i, n: (i, 0)),
                  pl.BlockSpec((ts, LANE), lambda i, n: (i, 0))],
        out_specs=pl.BlockSpec((None, ts, LANE), lambda i, n: (n, i, 0)),
        out_shape=jax.ShapeDtypeStruct((nsl, s, LANE), BF16),
        compiler_params=_cparams(("parallel", "arbitrary"), est),
        name="gqa_qk_prep",
    )(z3, gains, cos, sin_signed)


def _softmax_update_t(k_tiles, vt_tiles, q_chunk, m_ref, acc_ref, bias_chunk=None, shift=None):
    n_chunks = m_ref.shape[1] // QCHUNK
    units = [(t, j) for t in range(len(k_tiles)) for j in range(n_chunks)]

    def scores(u):
        t, j = units[u]
        st = _qk_scores(k_tiles[t], q_chunk(j))
        return st if bias_chunk is None else st + bias_chunk(t, j)

    pending = [scores(u) for u in range(min(QK_AHEAD, len(units)))]
    for u, (t, j) in enumerate(units):
        cols = slice(j * QCHUNK, (j + 1) * QCHUNK)
        st = pending.pop(0)
        if u + QK_AHEAD < len(units):
            pending.append(scores(u + QK_AHEAD))
        m_prev = m_ref[:, cols]
        tile_max = jnp.max(st, axis=0, keepdims=True)
        if shift is None:
            m_new = jnp.maximum(m_prev, tile_max)
            sub = m_new
        else:
            m_new = jnp.maximum(m_prev, tile_max + shift)
            sub = m_new - shift
        alpha = jnp.exp2(m_prev - m_new)
        p = jnp.exp2(st - sub).astype(BF16)
        acc_ref[:, cols] = alpha * acc_ref[:, cols] + jnp.dot(vt_tiles[t], p, preferred_element_type=F32)
        m_ref[:, cols] = m_new


def _softmax_init(m_ref, acc_ref):
    m_ref[...] = jnp.full(m_ref.shape, -jnp.inf, F32)
    acc_ref[...] = jnp.zeros(acc_ref.shape, F32)


def _softmax_result_t(acc_ref):
    acc = acc_ref[...]
    return acc[0:HEAD_DIM] / acc[HEAD_DIM:HEAD_DIM + 1]


def _transposed_v(v, tk):
    n, s, _ = v.shape
    vt = jnp.swapaxes(v.reshape(n, s // tk, tk, HEAD_DIM), 2, 3)
    ones = jnp.ones((n, s // tk, VT_ROWS - HEAD_DIM, tk), BF16)
    return jnp.concatenate([vt, ones], axis=2)


def _qk_scores(a, b):
    return lax.dot_general(a, b, (((1,), (1,)), ((), ())), preferred_element_type=F32)


def _gqa_kernel(q_ref, k_ref, vt_ref, o_ref, m_ref, acc_ref):
    tq = q_ref.shape[1]
    n_k, _, tk = vt_ref.shape
    per_head = tq // QCHUNK

    def q_chunk(j):
        r0 = (j % per_head) * QCHUNK
        return q_ref[j // per_head, r0:r0 + QCHUNK, :]

    _softmax_init(m_ref, acc_ref)

    def key_tiles(kg, carry):
        tiles = [kg * KV_UNROLL + t for t in range(KV_UNROLL)]
        _softmax_update_t([k_ref[pl.ds(pl.multiple_of(ki * tk, tk), tk), :] for ki in tiles],
                          [vt_ref[ki] for ki in tiles], q_chunk, m_ref, acc_ref)
        return carry

    lax.fori_loop(0, n_k // KV_UNROLL, key_tiles, 0)
    ot = _softmax_result_t(acc_ref)
    for g in range(B_GROUP):
        o_ref[:, g * HEAD_DIM:(g + 1) * HEAD_DIM] = ot[:, g * tq:(g + 1) * tq].T.astype(o_ref.dtype)


def _gqa(qk, vt):
    s = qk.shape[1]
    tq = TQ_GQA
    n_k, _, tk = vt.shape[1:]
    assert n_k % KV_UNROLL == 0
    cols = B_GROUP * tq
    est = (2 * (2 * _nbytes((cols, LANE), BF16) + _nbytes((s, LANE), BF16) + _nbytes((VT_ROWS, s), BF16))
           + 3 * _nbytes((VT_ROWS, cols), F32) + 2 * (QK_AHEAD + 2) * _nbytes((tk, QCHUNK), F32))
    return pl.pallas_call(
        _gqa_kernel,
        grid=(B_KV_HEADS, s // tq),
        in_specs=[pl.BlockSpec((B_GROUP, tq, LANE), lambda h, i: (h, i, 0)),
                  pl.BlockSpec((None, s, LANE), lambda h, i: (B_HEADS + h, 0, 0)),
                  pl.BlockSpec((None, n_k, VT_ROWS, tk), lambda h, i: (h, 0, 0, 0))],
        out_specs=pl.BlockSpec((tq, B_GROUP * HEAD_DIM), lambda h, i: (i, h)),
        out_shape=jax.ShapeDtypeStruct((s, B_WIDTH), BF16),
        scratch_shapes=[pltpu.VMEM((1, cols), F32), pltpu.VMEM((VT_ROWS, cols), F32)],
        compiler_params=_cparams(("parallel", "parallel"), est),
        name="gqa_attention",
    )(qk, qk, vt)


def _t5_bias_tile_t(t5_ref, h, key_offset, tk, tq):
    kpos = lax.broadcasted_iota(jnp.int32, (tk, tq), 0) + key_offset
    qpos = lax.broadcasted_iota(jnp.int32, (tk, tq), 1)
    rel = kpos - qpos
    nb = T5_BUCKETS // 2
    max_exact = nb // 2
    ret = jnp.where(rel > 0, nb, 0)
    n = jnp.abs(rel)
    nf = jnp.maximum(n, 1).astype(F32)
    large = max_exact + (jnp.log(nf / max_exact) / math.log(T5_MAX_DIST / max_exact)
                         * (nb - max_exact)).astype(jnp.int32)
    large = jnp.minimum(large, nb - 1)
    bucket = ret + jnp.where(n < max_exact, n, large)
    bias = jnp.full((tk, tq), t5_ref[h], F32)
    for b in range(1, T5_BUCKETS):
        bias = jnp.where(bucket == b, t5_ref[b * A_HEADS + h], bias)
    return bias * LOG2E


def _diff_kernel(t5_ref, lam_ref, q_ref, k_ref, vt_ref, g_ref, o_ref, qs_ref, bias_ref, m_ref, acc_ref):
    h, qi = pl.program_id(0), pl.program_id(1)
    tq = q_ref.shape[0]
    n_k, _, tk = vt_ref.shape
    n_sub = tq // tk
    n_near = n_sub + 2
    per_map = tq // QCHUNK

    @pl.when(qi == 0)
    def _():
        far_left = t5_ref[(T5_BUCKETS // 2 - 1) * A_HEADS + h] * LOG2E
        far_right = t5_ref[(T5_BUCKETS - 1) * A_HEADS + h] * LOG2E
        bias_ref[0] = jnp.full((tk, tq), far_left, F32)
        for o in range(n_near):
            bias_ref[o + 1] = _t5_bias_tile_t(t5_ref, h, (o - 1) * tk, tk, tq)
        bias_ref[n_near + 1] = jnp.full((tk, tq), far_right, F32)

    _softmax_init(m_ref, acc_ref)
    q = q_ref[...]
    lane = lax.broadcasted_iota(jnp.int32, q.shape, 1)
    zero = jnp.zeros_like(q)
    qs_ref[0:tq, :] = jnp.where(lane < A_QK_DIM, q, zero)
    qs_ref[tq:2 * tq, :] = jnp.where(lane >= A_QK_DIM, q, zero)

    def q_chunk(j):
        return qs_ref[j * QCHUNK:(j + 1) * QCHUNK, :]

    def tile_operands(kg):
        tiles = [kg * KV_UNROLL + t for t in range(KV_UNROLL)]
        return (tiles, [k_ref[pl.ds(pl.multiple_of(ki * tk, tk), tk), :] for ki in tiles],
                [vt_ref[ki] for ki in tiles])

    def near_tiles(kg, carry):
        tiles, k_tiles, vt_tiles = tile_operands(kg)
        bias_idx = [jnp.clip(ki - n_sub * qi + 2, 0, n_near + 1) for ki in tiles]

        def bias_chunk(t, j):
            c0 = (j % per_map) * QCHUNK
            return bias_ref[bias_idx[t], :, c0:c0 + QCHUNK]

        _softmax_update_t(k_tiles, vt_tiles, q_chunk, m_ref, acc_ref, bias_chunk=bias_chunk)
        return carry

    def far_tiles(kg, c):
        _, k_tiles, vt_tiles = tile_operands(kg)
        _softmax_update_t(k_tiles, vt_tiles, q_chunk, m_ref, acc_ref, shift=c)
        return c

    n_groups = n_k // KV_UNROLL
    near_lo = jnp.maximum(n_sub * qi - 1, 0) // KV_UNROLL
    near_hi = jnp.minimum((n_sub * qi + n_sub) // KV_UNROLL + 1, n_groups)
    lax.fori_loop(0, near_lo, far_tiles, t5_ref[(T5_BUCKETS // 2 - 1) * A_HEADS + h] * LOG2E)
    lax.fori_loop(near_lo, near_hi, near_tiles, 0)
    lax.fori_loop(near_hi, n_groups, far_tiles, t5_ref[(T5_BUCKETS - 1) * A_HEADS + h] * LOG2E)
    ot = _softmax_result_t(acc_ref)
    out = (ot[:, 0:tq] - lam_ref[0] * ot[:, tq:2 * tq]).T
    o_ref[...] = (_rmsnorm_rows(out, g_ref[...]) * lam_ref[1]).astype(o_ref.dtype)


def _diff_attention(z3, vt, t5_flat, lam_pair, subln_g):
    s = z3.shape[1]
    tq = TQ_DIFF
    n_k, _, tk = vt.shape[1:]
    assert tk >= T5_MAX_DIST and tq % tk == 0 and s % tq == 0 and tq % QCHUNK == 0 and n_k % KV_UNROLL == 0
    n_near = tq // tk + 2
    est = (2 * (2 * _nbytes((tq, LANE), BF16) + _nbytes((s, LANE), BF16) + _nbytes((VT_ROWS, s), BF16))
           + _nbytes((2 * tq, LANE), BF16) + (n_near + 2) * _nbytes((tk, tq), F32)
           + 3 * _nbytes((VT_ROWS, 2 * tq), F32) + 2 * (QK_AHEAD + 2) * _nbytes((tk, QCHUNK), F32))
    grid_spec = pltpu.PrefetchScalarGridSpec(
        num_scalar_prefetch=2,
        grid=(A_HEADS, s // tq),
        in_specs=[pl.BlockSpec((None, tq, LANE), lambda h, i, *_: (SLAB_QA + h, i, 0)),
                  pl.BlockSpec((None, s, LANE), lambda h, i, *_: (SLAB_KA + h, 0, 0)),
                  pl.BlockSpec((None, n_k, VT_ROWS, tk), lambda h, i, *_: (h, 0, 0, 0)),
                  pl.BlockSpec((1, LANE), lambda h, i, *_: (0, 0))],
        out_specs=pl.BlockSpec((tq, HEAD_DIM), lambda h, i, *_: (i, h)),
        scratch_shapes=[pltpu.VMEM((2 * tq, LANE), BF16), pltpu.VMEM((n_near + 2, tk, tq), F32),
                        pltpu.VMEM((1, 2 * tq), F32), pltpu.VMEM((VT_ROWS, 2 * tq), F32)])
    return pl.pallas_call(
        _diff_kernel,
        grid_spec=grid_spec,
        out_shape=jax.ShapeDtypeStruct((s, A_WIDTH), BF16),
        compiler_params=_cparams(("arbitrary", "arbitrary"), est),
        name="diff_attention",
    )(t5_flat, lam_pair, z3, z3, vt, subln_g.reshape(1, LANE))


def _na_window_start(first_row, rows):
    return max(0, min(first_row - NA_ROWS // 2, rows - NA_WIN_ROWS))


def _na_build_bias(rpb_ref, h, bias_ref, rows):
    cq = lax.broadcasted_iota(jnp.int32, (GRID_W, LANE), 0)
    ck = lax.broadcasted_iota(jnp.int32, (GRID_W, LANE), 1) % GRID_W
    cs = jnp.clip(cq - NA_COLS // 2, 0, GRID_W - NA_COLS)
    col_valid = (ck >= cs) & (ck < cs + NA_COLS)
    coff = jnp.clip(ck - cq + (NA_COLS - 1), 0, NA_COFF - 1)
    lower_half = lax.broadcasted_iota(jnp.int32, (GRID_W, LANE), 1) < GRID_W
    masked = jnp.full((GRID_W, LANE), MASK_VALUE, F32)
    col_bias = []
    for ro in range(NA_ROFF):
        base = (h * NA_ROFF + ro) * NA_COFF
        b = jnp.full((GRID_W, LANE), rpb_ref[base], F32)
        for c in range(1, NA_COFF):
            b = jnp.where(coff == c, rpb_ref[base + c], b)
        col_bias.append(jnp.where(col_valid, b * LOG2E, MASK_VALUE))
    n_groups = rows // NA_GROUP_ROWS
    for var, group in enumerate((0, 1, n_groups - 1)):
        ws = _na_window_start(group * NA_GROUP_ROWS, rows)
        for a in range(NA_GROUP_ROWS):
            r = group * NA_GROUP_ROWS + a
            rs = max(0, min(r - NA_ROWS // 2, rows - NA_ROWS))
            for ip in range(NA_WIN_ROWS // 2):
                halves = []
                for kr in (ws + 2 * ip, ws + 2 * ip + 1):
                    valid = rs <= kr < rs + NA_ROWS
                    halves.append(col_bias[kr - r + NA_ROWS - 1] if valid else masked)
                bias_ref[var, a * GRID_W:(a + 1) * GRID_W, ip * LANE:(ip + 1) * LANE] = (
                    jnp.where(lower_half, halves[0], halves[1]))


def _na_kernel(rpb_ref, q_ref, k_ref, v_ref, o_ref, bias_ref, *, rows):
    h, g = pl.program_id(0), pl.program_id(1)
    n_groups = rows // NA_GROUP_ROWS
    win = NA_WIN_ROWS * GRID_W

    @pl.when(g == 0)
    def _():
        _na_build_bias(rpb_ref, h, bias_ref, rows)

    ws = jnp.clip(g * NA_GROUP_ROWS - NA_ROWS // 2, 0, rows - NA_WIN_ROWS)
    start = pl.multiple_of(ws * GRID_W, (NA_ROWS // 2) * GRID_W)
    kw = k_ref[pl.ds(start, win), :]
    vw = v_ref[pl.ds(start, win), :]
    var = jnp.where(g == 0, 0, jnp.where(g == n_groups - 1, 2, 1))
    s = _qk_scores(q_ref[...], kw) + bias_ref[var]
    m = jnp.max(s, axis=-1, keepdims=True)
    p = jnp.exp2(s - m)
    l = jnp.sum(p, axis=-1, keepdims=True)
    o = jnp.dot(p.astype(BF16), vw, preferred_element_type=F32) / l
    o_ref[...] = o.astype(o_ref.dtype)


def _neighbourhood_attention(z3, rpb_flat):
    s = z3.shape[1]
    rows = s // GRID_W
    assert rows % NA_GROUP_ROWS == 0 and rows >= 3 * NA_GROUP_ROWS
    tq = NA_GROUP_ROWS * GRID_W
    win = NA_WIN_ROWS * GRID_W
    est = (2 * (2 * _nbytes((tq, LANE), BF16) + 2 * _nbytes((s, LANE), BF16))
           + 3 * _nbytes((tq, win), F32) + 3 * _nbytes((tq, win), F32))
    grid_spec = pltpu.PrefetchScalarGridSpec(
        num_scalar_prefetch=1,
        grid=(C_HEADS, rows // NA_GROUP_ROWS),
        in_specs=[pl.BlockSpec((None, tq, LANE), lambda h, g, *_: (SLAB_QC + h, g, 0)),
                  pl.BlockSpec((None, s, LANE), lambda h, g, *_: (SLAB_KC + h, 0, 0)),
                  pl.BlockSpec((None, s, LANE), lambda h, g, *_: (SLAB_VC + h, 0, 0))],
        out_specs=pl.BlockSpec((tq, HEAD_DIM), lambda h, g, *_: (g, h)),
        scratch_shapes=[pltpu.VMEM((3, tq, win), F32)])
    return pl.pallas_call(
        functools.partial(_na_kernel, rows=rows),
        grid_spec=grid_spec,
        out_shape=jax.ShapeDtypeStruct((s, C_WIDTH), BF16),
        compiler_params=_cparams(("arbitrary", "arbitrary"), est),
        name="neighbourhood_attention",
    )(rpb_flat, z3, z3, z3)


def _router_kernel(x_ref, g_ref, wr_ref, o_ref):
    h = _rmsnorm_rows(x_ref[...], g_ref[...])
    logits = jnp.dot(h, wr_ref[...], preferred_element_type=F32, precision=lax.Precision.HIGHEST)
    lane = lax.broadcasted_iota(jnp.int32, logits.shape, 1)
    lane_f = lane.astype(F32)
    logits = jnp.where(lane < N_EXPERTS, logits, -jnp.inf)
    m1 = jnp.max(logits, axis=-1, keepdims=True)
    i1 = jnp.min(jnp.where(logits == m1, lane_f, float(LANE)), axis=-1, keepdims=True)
    rest = jnp.where(lane_f == i1, -jnp.inf, logits)
    m2 = jnp.max(rest, axis=-1, keepdims=True)
    i2 = jnp.min(jnp.where(rest == m2, lane_f, float(LANE)), axis=-1, keepdims=True)
    e = jnp.exp(m2 - m1)
    g1 = 1.0 / (1.0 + e)
    g2 = e / (1.0 + e)
    o_ref[...] = jnp.where(lane == 0, i1, jnp.where(lane == 1, i2, jnp.where(lane == 2, g1, g2)))


def _router(x, g, w_router):
    s, d = x.shape
    tm = TM_ROUTE
    wr = jnp.zeros((d, LANE), F32).at[:, :N_EXPERTS].set(w_router)
    est = 2 * (_nbytes((tm, d), F32) + _nbytes((d, LANE), F32) + _nbytes((tm, LANE), F32)) + 4 * _nbytes((tm, d), F32)
    return pl.pallas_call(
        _router_kernel,
        grid=(s // tm,),
        in_specs=[pl.BlockSpec((tm, d), lambda i: (i, 0)),
                  pl.BlockSpec((1, d), lambda i: (0, 0)),
                  pl.BlockSpec((d, LANE), lambda i: (0, 0))],
        out_specs=pl.BlockSpec((tm, LANE), lambda i: (i, 0)),
        out_shape=jax.ShapeDtypeStruct((s, LANE), F32),
        compiler_params=_cparams(("parallel",), est),
        name="moe_router",
    )(x, g.reshape(1, d), wr)


def _row_gather(src_hbm, dst_ref, sem, n_rows, row_of):
    def issue(r, carry):
        pltpu.make_async_copy(src_hbm.at[pl.ds(row_of(r), 1)], dst_ref.at[pl.ds(r, 1)], sem).start()
        return carry
    lax.fori_loop(0, n_rows, issue, 0, unroll=DMA_ISSUE_UNROLL)
    pltpu.make_async_copy(src_hbm.at[pl.ds(0, n_rows)], dst_ref.at[pl.ds(0, n_rows)], sem).wait()


def _moe_kernel(tile_e_ref, nused_ref, tok_ref, x_hbm, gn_ref, wg_ref, wu_ref, wd_ref,
                o_ref, xg_ref, xb_ref, acc_ref, sem):
    i, f = pl.program_id(0), pl.program_id(1)
    tm = xg_ref.shape[0]
    used = i < nused_ref[0]
    last = f == pl.num_programs(1) - 1

    @pl.when(used & (f == 0))
    def _():
        _row_gather(x_hbm, xg_ref, sem, tm, lambda r: tok_ref[i * tm + r])
        xb_ref[...] = _rmsnorm_rows(xg_ref[...], gn_ref[...]).astype(BF16)
        acc_ref[...] = jnp.zeros(acc_ref.shape, F32)

    @pl.when(used)
    def _():
        hb = xb_ref[...]
        g = jnp.dot(hb, wg_ref[...], preferred_element_type=F32)
        u = jnp.dot(hb, wu_ref[...], preferred_element_type=F32)
        act = (g * _sigmoid(g) * u).astype(BF16)
        acc_ref[...] += jnp.dot(act, wd_ref[...], preferred_element_type=F32)

    @pl.when(used & last)
    def _():
        o_ref[...] = acc_ref[...]

    @pl.when(jnp.logical_not(used) & last)
    def _():
        o_ref[...] = jnp.zeros(o_ref.shape, F32)


def _moe_experts(x, gn, wg, wu, wd, tile_e, n_used, buf_tok):
    d = x.shape[1]
    fdim = wg.shape[2]
    cap = buf_tok.shape[0]
    tm, tf = TM_MOE, TF_MOE
    nf = fdim // tf

    def fsel(i, f, nused):
        return jnp.where(i < nused[0], f, nf - 1)

    est = (2 * (3 * _nbytes((d, tf), BF16) + _nbytes((tm, d), F32))
           + 2 * _nbytes((tm, d), F32) + _nbytes((tm, d), BF16)
           + 3 * _nbytes((tm, tf), F32) + _nbytes((tm, d), F32))
    grid_spec = pltpu.PrefetchScalarGridSpec(
        num_scalar_prefetch=3,
        grid=(cap // tm, nf),
        in_specs=[pl.BlockSpec(memory_space=pl.ANY),
                  pl.BlockSpec((1, d), lambda i, f, te, nu, tok: (0, 0)),
                  pl.BlockSpec((None, d, tf), lambda i, f, te, nu, tok: (te[i], 0, fsel(i, f, nu))),
                  pl.BlockSpec((None, d, tf), lambda i, f, te, nu, tok: (te[i], 0, fsel(i, f, nu))),
                  pl.BlockSpec((None, tf, d), lambda i, f, te, nu, tok: (te[i], fsel(i, f, nu), 0))],
        out_specs=pl.BlockSpec((tm, d), lambda i, f, te, nu, tok: (i, 0)),
        scratch_shapes=[pltpu.VMEM((tm, d), F32), pltpu.VMEM((tm, d), BF16),
                        pltpu.VMEM((tm, d), F32), pltpu.SemaphoreType.DMA(())])
    return pl.pallas_call(
        _moe_kernel,
        grid_spec=grid_spec,
        out_shape=jax.ShapeDtypeStruct((cap, d), F32),
        compiler_params=_cparams(("arbitrary", "arbitrary"), est),
        name="moe_experts",
    )(tile_e, n_used, buf_tok, x, gn.reshape(1, d), wg, wu, wd)


def _combine_kernel(pos_ref, x_ref, route_ref, y_hbm, gf_ref, o_ref, buf_ref, sem, *, final_norm):
    i = pl.program_id(0)
    tm = x_ref.shape[0]
    slot = i % 2

    def start_gather(step, dst_slot):
        def issue(r, carry):
            row = pos_ref[step * (TOP_K * tm) + r]
            pltpu.make_async_copy(y_hbm.at[pl.ds(row, 1)], buf_ref.at[dst_slot, pl.ds(r, 1)],
                                  sem.at[dst_slot]).start()
            return carry
        lax.fori_loop(0, TOP_K * tm, issue, 0, unroll=DMA_ISSUE_UNROLL)

    @pl.when(i == 0)
    def _():
        start_gather(0, 0)

    @pl.when(i + 1 < pl.num_programs(0))
    def _():
        start_gather(i + 1, 1 - slot)

    pltpu.make_async_copy(y_hbm.at[pl.ds(0, TOP_K * tm)], buf_ref.at[slot], sem.at[slot]).wait()
    route = route_ref[...]
    y = (route[:, TOP_K:TOP_K + 1] * buf_ref[slot, 0:tm, :]
         + route[:, TOP_K + 1:TOP_K + 2] * buf_ref[slot, tm:2 * tm, :])
    out = x_ref[...] + y
    if final_norm:
        out = _rmsnorm_rows(out, gf_ref[...])
    o_ref[...] = out


def _moe_combine(x, yb, pos, route, final_gain):
    s, d = x.shape
    tm = TM_COMB
    pos_flat = jnp.swapaxes(pos.reshape(s // tm, tm, TOP_K), 1, 2).reshape(-1)
    final_norm = final_gain is not None
    gf = final_gain if final_norm else jnp.ones((d,), F32)
    est = 2 * (2 * _nbytes((tm, d), F32) + _nbytes((tm, LANE), F32)) + 2 * _nbytes((TOP_K * tm, d), F32) \
        + 3 * _nbytes((tm, d), F32)
    grid_spec = pltpu.PrefetchScalarGridSpec(
        num_scalar_prefetch=1,
        grid=(s // tm,),
        in_specs=[pl.BlockSpec((tm, d), lambda i, pos: (i, 0)),
                  pl.BlockSpec((tm, LANE), lambda i, pos: (i, 0)),
                  pl.BlockSpec(memory_space=pl.ANY),
                  pl.BlockSpec((1, d), lambda i, pos: (0, 0))],
        out_specs=pl.BlockSpec((tm, d), lambda i, pos: (i, 0)),
        scratch_shapes=[pltpu.VMEM((2, TOP_K * tm, d), F32), pltpu.SemaphoreType.DMA((2,))])
    return pl.pallas_call(
        functools.partial(_combine_kernel, final_norm=final_norm),
        grid_spec=grid_spec,
        out_shape=jax.ShapeDtypeStruct((s, d), F32),
        compiler_params=_cparams(("arbitrary",), est),
        name="moe_combine",
    )(pos_flat, x, route, yb, gf.reshape(1, d))


def _moe_layer(x, gn, w_router, wg, wu, wd, final_gain=None):
    s = x.shape[0]
    tm = TM_MOE
    route = _router(x, gn, w_router)
    e_flat = route[:, 0:TOP_K].astype(jnp.int32).reshape(-1)
    n_assign = s * TOP_K
    onehot = (e_flat[:, None] == jnp.arange(N_EXPERTS, dtype=jnp.int32)[None, :]).astype(jnp.int32)
    csum = jnp.cumsum(onehot, axis=0)
    rank = jnp.sum(csum * onehot, axis=1) - 1
    counts = csum[-1]
    pcounts = (counts + tm - 1) // tm * tm
    pends = jnp.cumsum(pcounts)
    pstarts = pends - pcounts
    dest = jnp.sum(pstarts[None, :] * onehot, axis=1) + rank
    cap = n_assign + N_EXPERTS * tm
    tok_flat = jnp.arange(n_assign, dtype=jnp.int32) // TOP_K
    buf_tok = jnp.zeros((cap,), jnp.int32).at[dest].set(tok_flat)
    n_tiles = cap // tm
    tile_e = jnp.minimum(jnp.searchsorted(pends, jnp.arange(n_tiles, dtype=jnp.int32) * tm, side='right'),
                         N_EXPERTS - 1).astype(jnp.int32)
    n_used = (pends[-1:] // tm).astype(jnp.int32)
    yb = _moe_experts(x, gn, wg, wu, wd, tile_e, n_used, buf_tok)
    return _moe_combine(x, yb, dest.astype(jnp.int32).reshape(s, TOP_K), route, final_gain)


def kernel(x, w_in, w_branch_a, w_branch_b, w_branch_c, w_out, norm_mix, norm_ffn, norm_final, t5_bias, diff_lambda, diff_subln, qk_norm_b, na_rpb, ffn_gate, ffn_up, ffn_down, moe_router, moe_gate, moe_up, moe_down):
    bn, s, d = x.shape
    assert bn == 1 and d == D_MODEL
    depth = w_in.shape[0]
    xs = x.reshape(s, d)
    cos, sin_signed = _rope_tables(s)
    t5_flat = t5_bias.astype(F32).reshape(-1)
    colscale = jnp.ones((1, QKV_WIDTH), F32)
    colscale = colscale.at[:, SLAB_QA * LANE:SLAB_KA * LANE].set(A_QK_DIM ** -0.5 * LOG2E)
    colscale = colscale.at[:, SLAB_QC * LANE:SLAB_KC * LANE].set(HEAD_DIM ** -0.5 * LOG2E)

    for l in range(depth):
        h = _rmsnorm(xs, norm_mix[l], BF16)
        w_l = w_in[l].astype(BF16)
        z3 = _proj_slabs(h, w_l, colscale)
        gates = _matmul_sigmoid(h, w_l, QKV_WIDTH)

        lam_init = 0.8 - 0.6 * math.exp(-0.3 * l)
        lp = diff_lambda[l].astype(F32)
        lam = jnp.exp(jnp.sum(lp[0] * lp[1])) - jnp.exp(jnp.sum(lp[2] * lp[3])) + lam_init
        lam_pair = jnp.stack([lam, jnp.asarray(1.0 - lam_init, F32)]).astype(F32)
        vt_a = _transposed_v(z3[SLAB_VA:SLAB_VA + A_HEADS], TK_DIFF)
        vt_b = _transposed_v(z3[SLAB_VB:SLAB_VB + B_KV_HEADS], TK_GQA)
        ya = _diff_attention(z3, vt_a, t5_flat, lam_pair, diff_subln[l].astype(F32))

        q_gain = qk_norm_b[l, 0].astype(F32) * (HEAD_DIM ** -0.5 * LOG2E)
        k_gain = qk_norm_b[l, 1].astype(F32)
        gains = jnp.concatenate([jnp.tile(q_gain[None], (B_HEADS, 1)),
                                 jnp.tile(k_gain[None], (B_KV_HEADS, 1))])[:, None, :]
        yb = _gqa(_qk_prep(z3, gains, cos, sin_signed), vt_b)

        yc = _neighbourhood_attention(z3, na_rpb[l].astype(F32).reshape(-1))

        merged = _merge(ya, yb, yc, w_branch_a[l].astype(BF16), w_branch_b[l].astype(BF16),
                        w_branch_c[l].astype(BF16), gates)
        xs = _matmul_residual(merged, w_out[l].astype(BF16), xs, TM_OUT, TN_OUT, "out_proj")

        j = l // 2
        if l % 2 == 0:
            hf = _rmsnorm(xs, norm_ffn[l], BF16)
            act = _glu(hf, ffn_gate[j].astype(BF16), ffn_up[j].astype(BF16))
            xs = _matmul_residual(act, ffn_down[j].astype(BF16), xs, TM_DOWN, TN_DOWN, "ffn_down")
        else:
            last = l == depth - 1
            xs = _moe_layer(xs, norm_ffn[l], moe_router[j], moe_gate[j].astype(BF16),
                            moe_up[j].astype(BF16), moe_down[j].astype(BF16),
                            final_gain=norm_final.astype(F32) if last else None)
            if last:
                return xs.astype(x.dtype).reshape(bn, s, d)
    return _rmsnorm(xs, norm_final, x.dtype).reshape(bn, s, d)
```

```python
import functools
import math

import jax
import jax.numpy as jnp
from jax import lax
from jax.experimental import pallas as pl
from jax.experimental.pallas import tpu as pltpu

F32 = jnp.float32
BF16 = jnp.bfloat16

D_MODEL = 2048
GRID_W = 64
HEAD_DIM = 128
NORM_EPS = 1e-6
A_HEADS = 4
A_QK_DIM = 64
A_WIDTH = A_HEADS * HEAD_DIM
B_HEADS = 8
B_KV_HEADS = 2
B_GROUP = B_HEADS // B_KV_HEADS
B_WIDTH = B_HEADS * HEAD_DIM
C_HEADS = 4
C_WIDTH = C_HEADS * HEAD_DIM
NA_ROWS = 8
NA_COLS = 16
NA_ROFF = 2 * NA_ROWS - 1
NA_COFF = 2 * NA_COLS - 1
T5_BUCKETS = 32
T5_MAX_DIST = 128
ROPE_THETA = 10000.0
N_EXPERTS = 8
TOP_K = 2
QKV_WIDTH = 2 * A_WIDTH + A_WIDTH + B_WIDTH + 2 * B_KV_HEADS * HEAD_DIM + 3 * C_WIDTH
LOG2E = 1.4426950408889634
MASK_VALUE = -1e30

LANE = 128
V7X_VMEM_BYTES = 64 * 1024 * 1024
VMEM_CAP_BYTES = 56 * 1024 * 1024
QCHUNK = 256

SLAB_QA, SLAB_KA, SLAB_VA = 0, 4, 8
SLAB_QB, SLAB_KB, SLAB_VB = 12, 20, 22
SLAB_QC, SLAB_KC, SLAB_VC = 24, 28, 32
N_SLABS = QKV_WIDTH // LANE
VT_ROWS = HEAD_DIM + 16

TM_NORM = 512
TM_PROJ, TN_PROJ = 1024, 1536
TS_PREP = 4096
TQ_DIFF, TK_DIFF = 1024, 512
TQ_GQA, TK_GQA = 512, 512
KV_UNROLL = 4
QK_AHEAD = 3
BIAS_BLOCK = 128
NA_GROUP_ROWS = 8
NA_WIN_ROWS = 16
TM_MERGE, TN_MERGE = 512, 1024
TM_OUT, TN_OUT = 1024, 1024
TM_GLU, TF_GLU = 1024, 512
TM_DOWN, TN_DOWN = 1024, 512
TM_ROUTE = 512
TM_MOE, TF_MOE = 512, 1024
TM_COMB = 256
DMA_ISSUE_UNROLL = 8


def _cparams(semantics, vmem_bytes):
    return pltpu.CompilerParams(dimension_semantics=semantics,
                                vmem_limit_bytes=int(min(vmem_bytes, VMEM_CAP_BYTES)))


def _nbytes(shape, dtype):
    return math.prod(shape) * jnp.dtype(dtype).itemsize


def _rmsnorm_rows(x, g):
    ms = jnp.mean(x * x, axis=-1, keepdims=True)
    return x * lax.rsqrt(ms + NORM_EPS) * g


def _rmsnorm_kernel(x_ref, g_ref, o_ref):
    o_ref[...] = _rmsnorm_rows(x_ref[...], g_ref[...]).astype(o_ref.dtype)


def _rmsnorm(x, g, out_dtype):
    s, d = x.shape
    tm = TM_NORM
    est = 2 * (_nbytes((tm, d), F32) + _nbytes((tm, d), out_dtype)) + 3 * _nbytes((tm, d), F32)
    return pl.pallas_call(
        _rmsnorm_kernel,
        grid=(s // tm,),
        in_specs=[pl.BlockSpec((tm, d), lambda i: (i, 0)),
                  pl.BlockSpec((1, d), lambda i: (0, 0))],
        out_specs=pl.BlockSpec((tm, d), lambda i: (i, 0)),
        out_shape=jax.ShapeDtypeStruct((s, d), out_dtype),
        compiler_params=_cparams(("parallel",), est),
        name="rmsnorm",
    )(x, g.reshape(1, d))


def _proj_slab_kernel(a_ref, w_ref, cs_ref, o_ref):
    acc = jnp.dot(a_ref[...], w_ref[...], preferred_element_type=F32) * cs_ref[...]
    for c in range(o_ref.shape[0]):
        o_ref[c] = acc[:, c * LANE:(c + 1) * LANE].astype(o_ref.dtype)


def _proj_slabs(a, w, colscale):
    s, k = a.shape
    n = colscale.shape[1]
    tm, tn = TM_PROJ, TN_PROJ
    spt = tn // LANE
    est = (2 * (_nbytes((tm, k), BF16) + _nbytes((k, tn), BF16) + _nbytes((tm, tn), BF16))
           + 2 * _nbytes((tm, tn), F32))
    return pl.pallas_call(
        _proj_slab_kernel,
        grid=(s // tm, n // tn),
        in_specs=[pl.BlockSpec((tm, k), lambda i, j: (i, 0)),
                  pl.BlockSpec((k, tn), lambda i, j: (0, j)),
                  pl.BlockSpec((1, tn), lambda i, j: (0, j))],
        out_specs=pl.BlockSpec((spt, tm, LANE), lambda i, j: (j, i, 0)),
        out_shape=jax.ShapeDtypeStruct((n // LANE, s, LANE), BF16),
        compiler_params=_cparams(("parallel", "arbitrary"), est),
        name="qkv_proj",
    )(a, w, colscale)


def _sigmoid(x):
    return 1.0 / (1.0 + jnp.exp(-x))


def _matmul_sigmoid_kernel(a_ref, w_ref, o_ref):
    acc = jnp.dot(a_ref[...], w_ref[...], preferred_element_type=F32)
    o_ref[...] = _sigmoid(acc).astype(o_ref.dtype)


def _matmul_sigmoid(a, w, col0):
    s, k = a.shape
    tm, tn = TM_PROJ, TN_PROJ
    n = w.shape[1] - col0
    assert col0 % tn == 0 and n % tn == 0
    j0 = col0 // tn
    est = (2 * (_nbytes((tm, k), BF16) + _nbytes((k, tn), BF16) + _nbytes((tm, tn), BF16))
           + 2 * _nbytes((tm, tn), F32))
    return pl.pallas_call(
        _matmul_sigmoid_kernel,
        grid=(s // tm, n // tn),
        in_specs=[pl.BlockSpec((tm, k), lambda i, j: (i, 0)),
                  pl.BlockSpec((k, tn), lambda i, j: (0, j0 + j))],
        out_specs=pl.BlockSpec((tm, tn), lambda i, j: (i, j)),
        out_shape=jax.ShapeDtypeStruct((s, n), BF16),
        compiler_params=_cparams(("parallel", "arbitrary"), est),
        name="gate_proj",
    )(a, w)


def _matmul_residual_kernel(a_ref, w_ref, r_ref, o_ref):
    o_ref[...] = r_ref[...] + jnp.dot(a_ref[...], w_ref[...], preferred_element_type=F32)


def _matmul_residual(a, w, r, tm, tn, name):
    s, k = a.shape
    n = w.shape[1]
    est = (2 * (_nbytes((tm, k), BF16) + _nbytes((k, tn), BF16) + 2 * _nbytes((tm, tn), F32))
           + _nbytes((tm, tn), F32))
    return pl.pallas_call(
        _matmul_residual_kernel,
        grid=(s // tm, n // tn),
        in_specs=[pl.BlockSpec((tm, k), lambda i, j: (i, 0)),
                  pl.BlockSpec((k, tn), lambda i, j: (0, j)),
                  pl.BlockSpec((tm, tn), lambda i, j: (i, j))],
        out_specs=pl.BlockSpec((tm, tn), lambda i, j: (i, j)),
        out_shape=jax.ShapeDtypeStruct((s, n), F32),
        compiler_params=_cparams(("parallel", "arbitrary"), est),
        name=name,
    )(a, w, r)


def _merge_kernel(ya_ref, yb_ref, yc_ref, wa_ref, wb_ref, wc_ref, ga_ref, gb_ref, gc_ref, o_ref):
    acc = ga_ref[...].astype(F32) * jnp.dot(ya_ref[...], wa_ref[...], preferred_element_type=F32)
    acc += gb_ref[...].astype(F32) * jnp.dot(yb_ref[...], wb_ref[...], preferred_element_type=F32)
    acc += gc_ref[...].astype(F32) * jnp.dot(yc_ref[...], wc_ref[...], preferred_element_type=F32)
    o_ref[...] = acc.astype(o_ref.dtype)


def _merge(ya, yb, yc, wa, wb, wc, gates):
    s = ya.shape[0]
    d = wa.shape[1]
    tm, tn = TM_MERGE, TN_MERGE
    nj = d // tn
    kin = A_WIDTH + B_WIDTH + C_WIDTH
    est = (2 * (_nbytes((tm, kin), BF16) + _nbytes((kin, tn), BF16) + 4 * _nbytes((tm, tn), BF16))
           + 4 * _nbytes((tm, tn), F32))
    return pl.pallas_call(
        _merge_kernel,
        grid=(s // tm, nj),
        in_specs=[pl.BlockSpec((tm, A_WIDTH), lambda i, j: (i, 0)),
                  pl.BlockSpec((tm, B_WIDTH), lambda i, j: (i, 0)),
                  pl.BlockSpec((tm, C_WIDTH), lambda i, j: (i, 0)),
                  pl.BlockSpec((A_WIDTH, tn), lambda i, j: (0, j)),
                  pl.BlockSpec((B_WIDTH, tn), lambda i, j: (0, j)),
                  pl.BlockSpec((C_WIDTH, tn), lambda i, j: (0, j)),
                  pl.BlockSpec((tm, tn), lambda i, j: (i, j)),
                  pl.BlockSpec((tm, tn), lambda i, j: (i, nj + j)),
                  pl.BlockSpec((tm, tn), lambda i, j: (i, 2 * nj + j))],
        out_specs=pl.BlockSpec((tm, tn), lambda i, j: (i, j)),
        out_shape=jax.ShapeDtypeStruct((s, d), BF16),
        compiler_params=_cparams(("parallel", "arbitrary"), est),
        name="branch_merge",
    )(ya, yb, yc, wa, wb, wc, gates, gates, gates)


def _glu_kernel(h_ref, wg_ref, wu_ref, o_ref):
    h = h_ref[...]
    g = jnp.dot(h, wg_ref[...], preferred_element_type=F32)
    u = jnp.dot(h, wu_ref[...], preferred_element_type=F32)
    o_ref[...] = (g * _sigmoid(g) * u).astype(o_ref.dtype)


def _glu(h, wg, wu):
    s, k = h.shape
    f = wg.shape[1]
    tm, tf = TM_GLU, TF_GLU
    est = (2 * (_nbytes((tm, k), BF16) + 2 * _nbytes((k, tf), BF16) + _nbytes((tm, tf), BF16))
           + 3 * _nbytes((tm, tf), F32))
    return pl.pallas_call(
        _glu_kernel,
        grid=(s // tm, f // tf),
        in_specs=[pl.BlockSpec((tm, k), lambda i, j: (i, 0)),
                  pl.BlockSpec((k, tf), lambda i, j: (0, j)),
                  pl.BlockSpec((k, tf), lambda i, j: (0, j))],
        out_specs=pl.BlockSpec((tm, tf), lambda i, j: (i, j)),
        out_shape=jax.ShapeDtypeStruct((s, f), BF16),
        compiler_params=_cparams(("parallel", "arbitrary"), est),
        name="ffn_glu",
    )(h, wg, wu)


def _rope_tables(s):
    half = HEAD_DIM // 2
    pos = jnp.arange(s, dtype=jnp.int32)
    row = (pos // GRID_W).astype(F32)
    col = (pos % GRID_W).astype(F32)
    inv = ROPE_THETA ** (-jnp.arange(0, half, 2, dtype=F32) / half)
    ang_r = row[:, None] * inv[None, :]
    ang_c = col[:, None] * inv[None, :]
    ang = jnp.concatenate([ang_r, ang_r, ang_c, ang_c], axis=-1)
    sign = jnp.tile(jnp.concatenate([-jnp.ones((half // 2,), F32), jnp.ones((half // 2,), F32)]), 2)
    return jnp.cos(ang), jnp.sin(ang) * sign[None, :]


def _qk_prep_kernel(z_ref, g_ref, cos_ref, sin_ref, o_ref):
    y = _rmsnorm_rows(z_ref[...].astype(F32), g_ref[...])
    lane = lax.broadcasted_iota(jnp.int32, y.shape, 1)
    quarter = HEAD_DIM // 4
    first = (lane % (2 * quarter)) < quarter
    partner = jnp.where(first, pltpu.roll(y, HEAD_DIM - quarter, 1), pltpu.roll(y, quarter, 1))
    o_ref[...] = (y * cos_ref[...] + partner * sin_ref[...]).astype(o_ref.dtype)


def _qk_prep(z3, gains, cos, sin_signed):
    s = z3.shape[1]
    ts = TS_PREP
    nsl = B_HEADS + B_KV_HEADS
    est = 2 * (2 * _nbytes((ts, LANE), BF16) + 2 * _nbytes((ts, LANE), F32)) + 6 * _nbytes((ts, LANE), F32)
    return pl.pallas_call(
        _qk_prep_kernel,
        grid=(s // ts, nsl),
        in_specs=[pl.BlockSpec((None, ts, LANE), lambda i, n: (SLAB_QB + n, i, 0)),
                  pl.BlockSpec((None, 1, LANE), lambda i, n: (n, 0, 0)),
                  pl.BlockSpec((ts, LANE), lambda i, n: (i, 0)),
                  pl.BlockSpec((ts, LANE), lambda i, n: (i, 0))],
        out_specs=pl.BlockSpec((None, ts, LANE), lambda i, n: (n, i, 0)),
        out_shape=jax.ShapeDtypeStruct((nsl, s, LANE), BF16),
        compiler_params=_cparams(("parallel", "arbitrary"), est),
        name="gqa_qk_prep",
    )(z3, gains, cos, sin_signed)


def _softmax_update_t(k_tiles, vt_tiles, q_chunk, m_ref, acc_ref, bias_chunk=None, shift=None):
    n_chunks = m_ref.shape[1] // QCHUNK
    units = [(t, j) for t in range(len(k_tiles)) for j in range(n_chunks)]

    def scores(u):
        t, j = units[u]
        st = _qk_scores(k_tiles[t], q_chunk(j))
        return st if bias_chunk is None else st + bias_chunk(t, j)

    pending = [scores(u) for u in range(min(QK_AHEAD, len(units)))]
    for u, (t, j) in enumerate(units):
        cols = slice(j * QCHUNK, (j + 1) * QCHUNK)
        st = pending.pop(0)
        if u + QK_AHEAD < len(units):
            pending.append(scores(u + QK_AHEAD))
        m_prev = m_ref[:, cols]
        tile_max = jnp.max(st, axis=0, keepdims=True)
        if shift is None:
            m_new = jnp.maximum(m_prev, tile_max)
            sub = m_new
        else:
            m_new = jnp.maximum(m_prev, tile_max + shift)
            sub = m_new - shift
        alpha = jnp.exp2(m_prev - m_new)
        p = jnp.exp2(st - sub).astype(BF16)
        acc_ref[:, cols] = alpha * acc_ref[:, cols] + jnp.dot(vt_tiles[t], p, preferred_element_type=F32)
        m_ref[:, cols] = m_new


def _softmax_init(m_ref, acc_ref):
    m_ref[...] = jnp.full(m_ref.shape, -jnp.inf, F32)
    acc_ref[...] = jnp.zeros(acc_ref.shape, F32)


def _softmax_result_t(acc_ref):
    acc = acc_ref[...]
    return acc[0:HEAD_DIM] / acc[HEAD_DIM:HEAD_DIM + 1]


def _transposed_v(v, tk):
    n, s, _ = v.shape
    vt = jnp.swapaxes(v.reshape(n, s // tk, tk, HEAD_DIM), 2, 3)
    ones = jnp.ones((n, s // tk, VT_ROWS - HEAD_DIM, tk), BF16)
    return jnp.concatenate([vt, ones], axis=2)


def _qk_scores(a, b):
    return lax.dot_general(a, b, (((1,), (1,)), ((), ())), preferred_element_type=F32)


def _gqa_kernel(q_ref, k_ref, vt_ref, o_ref, m_ref, acc_ref):
    tq = q_ref.shape[1]
    n_k, _, tk = vt_ref.shape
    per_head = tq // QCHUNK

    def q_chunk(j):
        r0 = (j % per_head) * QCHUNK
        return q_ref[j // per_head, r0:r0 + QCHUNK, :]

    _softmax_init(m_ref, acc_ref)

    def key_tiles(kg, carry):
        tiles = [kg * KV_UNROLL + t for t in range(KV_UNROLL)]
        _softmax_update_t([k_ref[pl.ds(pl.multiple_of(ki * tk, tk), tk), :] for ki in tiles],
                          [vt_ref[ki] for ki in tiles], q_chunk, m_ref, acc_ref)
        return carry

    lax.fori_loop(0, n_k // KV_UNROLL, key_tiles, 0)
    ot = _softmax_result_t(acc_ref)
    for g in range(B_GROUP):
        o_ref[:, g * HEAD_DIM:(g + 1) * HEAD_DIM] = ot[:, g * tq:(g + 1) * tq].T.astype(o_ref.dtype)


def _gqa(qk, vt):
    s = qk.shape[1]
    tq = TQ_GQA
    n_k, _, tk = vt.shape[1:]
    assert n_k % KV_UNROLL == 0
    cols = B_GROUP * tq
    est = (2 * (2 * _nbytes((cols, LANE), BF16) + _nbytes((s, LANE), BF16) + _nbytes((VT_ROWS, s), BF16))
           + 3 * _nbytes((VT_ROWS, cols), F32) + 2 * (QK_AHEAD + 2) * _nbytes((tk, QCHUNK), F32))
    return pl.pallas_call(
        _gqa_kernel,
        grid=(B_KV_HEADS, s // tq),
        in_specs=[pl.BlockSpec((B_GROUP, tq, LANE), lambda h, i: (h, i, 0)),
                  pl.BlockSpec((None, s, LANE), lambda h, i: (B_HEADS + h, 0, 0)),
                  pl.BlockSpec((None, n_k, VT_ROWS, tk), lambda h, i: (h, 0, 0, 0))],
        out_specs=pl.BlockSpec((tq, B_GROUP * HEAD_DIM), lambda h, i: (i, h)),
        out_shape=jax.ShapeDtypeStruct((s, B_WIDTH), BF16),
        scratch_shapes=[pltpu.VMEM((1, cols), F32), pltpu.VMEM((VT_ROWS, cols), F32)],
        compiler_params=_cparams(("parallel", "parallel"), est),
        name="gqa_attention",
    )(qk, qk, vt)


def _t5_bias_block(t5_ref, h, rel0):
    shape = (BIAS_BLOCK, BIAS_BLOCK)
    rel = lax.broadcasted_iota(jnp.int32, shape, 0) - lax.broadcasted_iota(jnp.int32, shape, 1) + rel0
    nb = T5_BUCKETS // 2
    max_exact = nb // 2
    ret = jnp.where(rel > 0, nb, 0)
    n = jnp.abs(rel)
    nf = jnp.maximum(n, 1).astype(F32)
    large = max_exact + (jnp.log(nf / max_exact) / math.log(T5_MAX_DIST / max_exact)
                         * (nb - max_exact)).astype(jnp.int32)
    large = jnp.minimum(large, nb - 1)
    bucket = ret + jnp.where(n < max_exact, n, large)
    bias = jnp.full(shape, t5_ref[h], F32)
    for b in range(1, T5_BUCKETS):
        bias = jnp.where(bucket == b, t5_ref[b * A_HEADS + h], bias)
    return bias * LOG2E


def _t5_store_bias_tile(t5_ref, h, bias_ref, idx, key_offset, far_left, far_right):
    _, tk, tq = bias_ref.shape
    for rb in range(tk // BIAS_BLOCK):
        for cb in range(tq // BIAS_BLOCK):
            rel0 = key_offset + (rb - cb) * BIAS_BLOCK
            if rel0 - (BIAS_BLOCK - 1) >= T5_MAX_DIST:
                block = jnp.full((BIAS_BLOCK, BIAS_BLOCK), far_right, F32)
            elif rel0 + (BIAS_BLOCK - 1) <= -T5_MAX_DIST:
                block = jnp.full((BIAS_BLOCK, BIAS_BLOCK), far_left, F32)
            else:
                block = _t5_bias_block(t5_ref, h, rel0)
            bias_ref[idx, rb * BIAS_BLOCK:(rb + 1) * BIAS_BLOCK, cb * BIAS_BLOCK:(cb + 1) * BIAS_BLOCK] = block


def _diff_kernel(t5_ref, lam_ref, q_ref, k_ref, vt_ref, g_ref, o_ref, qs_ref, bias_ref, m_ref, acc_ref):
    h, qi = pl.program_id(0), pl.program_id(1)
    tq = q_ref.shape[0]
    n_k, _, tk = vt_ref.shape
    n_sub = tq // tk
    n_near = n_sub + 2
    per_map = tq // QCHUNK

    @pl.when(qi == 0)
    def _():
        far_left = t5_ref[(T5_BUCKETS // 2 - 1) * A_HEADS + h] * LOG2E
        far_right = t5_ref[(T5_BUCKETS - 1) * A_HEADS + h] * LOG2E
        bias_ref[0] = jnp.full((tk, tq), far_left, F32)
        for o in range(n_near):
            _t5_store_bias_tile(t5_ref, h, bias_ref, o + 1, (o - 1) * tk, far_left, far_right)
        bias_ref[n_near + 1] = jnp.full((tk, tq), far_right, F32)

    _softmax_init(m_ref, acc_ref)
    q = q_ref[...]
    lane = lax.broadcasted_iota(jnp.int32, q.shape, 1)
    zero = jnp.zeros_like(q)
    qs_ref[0:tq, :] = jnp.where(lane < A_QK_DIM, q, zero)
    qs_ref[tq:2 * tq, :] = jnp.where(lane >= A_QK_DIM, q, zero)

    def q_chunk(j):
        return qs_ref[j * QCHUNK:(j + 1) * QCHUNK, :]

    def tile_operands(kg):
        tiles = [kg * KV_UNROLL + t for t in range(KV_UNROLL)]
        return (tiles, [k_ref[pl.ds(pl.multiple_of(ki * tk, tk), tk), :] for ki in tiles],
                [vt_ref[ki] for ki in tiles])

    def near_tiles(kg, carry):
        tiles, k_tiles, vt_tiles = tile_operands(kg)
        bias_idx = [jnp.clip(ki - n_sub * qi + 2, 0, n_near + 1) for ki in tiles]

        def bias_chunk(t, j):
            c0 = (j % per_map) * QCHUNK
            return bias_ref[bias_idx[t], :, c0:c0 + QCHUNK]

        _softmax_update_t(k_tiles, vt_tiles, q_chunk, m_ref, acc_ref, bias_chunk=bias_chunk)
        return carry

    def far_tiles(kg, c):
        _, k_tiles, vt_tiles = tile_operands(kg)
        _softmax_update_t(k_tiles, vt_tiles, q_chunk, m_ref, acc_ref, shift=c)
        return c

    n_groups = n_k // KV_UNROLL
    near_lo = jnp.maximum(n_sub * qi - 1, 0) // KV_UNROLL
    near_hi = jnp.minimum((n_sub * qi + n_sub) // KV_UNROLL + 1, n_groups)
    lax.fori_loop(0, near_lo, far_tiles, t5_ref[(T5_BUCKETS // 2 - 1) * A_HEADS + h] * LOG2E)
    lax.fori_loop(near_lo, near_hi, near_tiles, 0)
    lax.fori_loop(near_hi, n_groups, far_tiles, t5_ref[(T5_BUCKETS - 1) * A_HEADS + h] * LOG2E)
    ot = _softmax_result_t(acc_ref)
    out = (ot[:, 0:tq] - lam_ref[0] * ot[:, tq:2 * tq]).T
    o_ref[...] = (_rmsnorm_rows(out, g_ref[...]) * lam_ref[1]).astype(o_ref.dtype)


def _diff_attention(z3, vt, t5_flat, lam_pair, subln_g):
    s = z3.shape[1]
    tq = TQ_DIFF
    n_k, _, tk = vt.shape[1:]
    assert tk >= T5_MAX_DIST and tq % tk == 0 and s % tq == 0 and tq % QCHUNK == 0 and n_k % KV_UNROLL == 0
    n_near = tq // tk + 2
    est = (2 * (2 * _nbytes((tq, LANE), BF16) + _nbytes((s, LANE), BF16) + _nbytes((VT_ROWS, s), BF16))
           + _nbytes((2 * tq, LANE), BF16) + (n_near + 2) * _nbytes((tk, tq), F32)
           + 3 * _nbytes((VT_ROWS, 2 * tq), F32) + 2 * (QK_AHEAD + 2) * _nbytes((tk, QCHUNK), F32))
    grid_spec = pltpu.PrefetchScalarGridSpec(
        num_scalar_prefetch=2,
        grid=(A_HEADS, s // tq),
        in_specs=[pl.BlockSpec((None, tq, LANE), lambda h, i, *_: (SLAB_QA + h, i, 0)),
                  pl.BlockSpec((None, s, LANE), lambda h, i, *_: (SLAB_KA + h, 0, 0)),
                  pl.BlockSpec((None, n_k, VT_ROWS, tk), lambda h, i, *_: (h, 0, 0, 0)),
                  pl.BlockSpec((1, LANE), lambda h, i, *_: (0, 0))],
        out_specs=pl.BlockSpec((tq, HEAD_DIM), lambda h, i, *_: (i, h)),
        scratch_shapes=[pltpu.VMEM((2 * tq, LANE), BF16), pltpu.VMEM((n_near + 2, tk, tq), F32),
                        pltpu.VMEM((1, 2 * tq), F32), pltpu.VMEM((VT_ROWS, 2 * tq), F32)])
    return pl.pallas_call(
        _diff_kernel,
        grid_spec=grid_spec,
        out_shape=jax.ShapeDtypeStruct((s, A_WIDTH), BF16),
        compiler_params=_cparams(("arbitrary", "arbitrary"), est),
        name="diff_attention",
    )(t5_flat, lam_pair, z3, z3, vt, subln_g.reshape(1, LANE))


def _na_window_start(first_row, rows):
    return max(0, min(first_row - NA_ROWS // 2, rows - NA_WIN_ROWS))


def _na_build_bias(rpb_ref, h, bias_ref, rows):
    cq = lax.broadcasted_iota(jnp.int32, (GRID_W, LANE), 0)
    ck = lax.broadcasted_iota(jnp.int32, (GRID_W, LANE), 1) % GRID_W
    cs = jnp.clip(cq - NA_COLS // 2, 0, GRID_W - NA_COLS)
    col_valid = (ck >= cs) & (ck < cs + NA_COLS)
    coff = jnp.clip(ck - cq + (NA_COLS - 1), 0, NA_COFF - 1)
    lower_half = lax.broadcasted_iota(jnp.int32, (GRID_W, LANE), 1) < GRID_W
    masked = jnp.full((GRID_W, LANE), MASK_VALUE, F32)
    col_bias = []
    for ro in range(NA_ROFF):
        base = (h * NA_ROFF + ro) * NA_COFF
        b = jnp.full((GRID_W, LANE), rpb_ref[base], F32)
        for c in range(1, NA_COFF):
            b = jnp.where(coff == c, rpb_ref[base + c], b)
        col_bias.append(jnp.where(col_valid, b * LOG2E, MASK_VALUE))
    n_groups = rows // NA_GROUP_ROWS
    for var, group in enumerate((0, 1, n_groups - 1)):
        ws = _na_window_start(group * NA_GROUP_ROWS, rows)
        for a in range(NA_GROUP_ROWS):
            r = group * NA_GROUP_ROWS + a
            rs = max(0, min(r - NA_ROWS // 2, rows - NA_ROWS))
            for ip in range(NA_WIN_ROWS // 2):
                halves = []
                for kr in (ws + 2 * ip, ws + 2 * ip + 1):
                    valid = rs <= kr < rs + NA_ROWS
                    halves.append(col_bias[kr - r + NA_ROWS - 1] if valid else masked)
                bias_ref[var, a * GRID_W:(a + 1) * GRID_W, ip * LANE:(ip + 1) * LANE] = (
                    jnp.where(lower_half, halves[0], halves[1]))


def _na_kernel(rpb_ref, q_ref, k_ref, v_ref, o_ref, bias_ref, *, rows):
    h, g = pl.program_id(0), pl.program_id(1)
    n_groups = rows // NA_GROUP_ROWS
    win = NA_WIN_ROWS * GRID_W

    @pl.when(g == 0)
    def _():
        _na_build_bias(rpb_ref, h, bias_ref, rows)

    ws = jnp.clip(g * NA_GROUP_ROWS - NA_ROWS // 2, 0, rows - NA_WIN_ROWS)
    start = pl.multiple_of(ws * GRID_W, (NA_ROWS // 2) * GRID_W)
    kw = k_ref[pl.ds(start, win), :]
    vw = v_ref[pl.ds(start, win), :]
    var = jnp.where(g == 0, 0, jnp.where(g == n_groups - 1, 2, 1))
    s = _qk_scores(q_ref[...], kw) + bias_ref[var]
    m = jnp.max(s, axis=-1, keepdims=True)
    p = jnp.exp2(s - m)
    l = jnp.sum(p, axis=-1, keepdims=True)
    o = jnp.dot(p.astype(BF16), vw, preferred_element_type=F32) / l
    o_ref[...] = o.astype(o_ref.dtype)


def _neighbourhood_attention(z3, rpb_flat):
    s = z3.shape[1]
    rows = s // GRID_W
    assert rows % NA_GROUP_ROWS == 0 and rows >= 3 * NA_GROUP_ROWS
    tq = NA_GROUP_ROWS * GRID_W
    win = NA_WIN_ROWS * GRID_W
    est = (2 * (2 * _nbytes((tq, LANE), BF16) + 2 * _nbytes((s, LANE), BF16))
           + 3 * _nbytes((tq, win), F32) + 3 * _nbytes((tq, win), F32))
    grid_spec = pltpu.PrefetchScalarGridSpec(
        num_scalar_prefetch=1,
        grid=(C_HEADS, rows // NA_GROUP_ROWS),
        in_specs=[pl.BlockSpec((None, tq, LANE), lambda h, g, *_: (SLAB_QC + h, g, 0)),
                  pl.BlockSpec((None, s, LANE), lambda h, g, *_: (SLAB_KC + h, 0, 0)),
                  pl.BlockSpec((None, s, LANE), lambda h, g, *_: (SLAB_VC + h, 0, 0))],
        out_specs=pl.BlockSpec((tq, HEAD_DIM), lambda h, g, *_: (g, h)),
        scratch_shapes=[pltpu.VMEM((3, tq, win), F32)])
    return pl.pallas_call(
        functools.partial(_na_kernel, rows=rows),
        grid_spec=grid_spec,
        out_shape=jax.ShapeDtypeStruct((s, C_WIDTH), BF16),
        compiler_params=_cparams(("arbitrary", "arbitrary"), est),
        name="neighbourhood_attention",
    )(rpb_flat, z3, z3, z3)


def _router_kernel(x_ref, g_ref, wr_ref, o_ref):
    h = _rmsnorm_rows(x_ref[...], g_ref[...])
    logits = jnp.dot(h, wr_ref[...], preferred_element_type=F32, precision=lax.Precision.HIGHEST)
    lane = lax.broadcasted_iota(jnp.int32, logits.shape, 1)
    lane_f = lane.astype(F32)
    logits = jnp.where(lane < N_EXPERTS, logits, -jnp.inf)
    m1 = jnp.max(logits, axis=-1, keepdims=True)
    i1 = jnp.min(jnp.where(logits == m1, lane_f, float(LANE)), axis=-1, keepdims=True)
    rest = jnp.where(lane_f == i1, -jnp.inf, logits)
    m2 = jnp.max(rest, axis=-1, keepdims=True)
    i2 = jnp.min(jnp.where(rest == m2, lane_f, float(LANE)), axis=-1, keepdims=True)
    e = jnp.exp(m2 - m1)
    g1 = 1.0 / (1.0 + e)
    g2 = e / (1.0 + e)
    o_ref[...] = jnp.where(lane == 0, i1, jnp.where(lane == 1, i2, jnp.where(lane == 2, g1, g2)))


def _router(x, g, w_router):
    s, d = x.shape
    tm = TM_ROUTE
    wr = jnp.zeros((d, LANE), F32).at[:, :N_EXPERTS].set(w_router)
    est = 2 * (_nbytes((tm, d), F32) + _nbytes((d, LANE), F32) + _nbytes((tm, LANE), F32)) + 4 * _nbytes((tm, d), F32)
    return pl.pallas_call(
        _router_kernel,
        grid=(s // tm,),
        in_specs=[pl.BlockSpec((tm, d), lambda i: (i, 0)),
                  pl.BlockSpec((1, d), lambda i: (0, 0)),
                  pl.BlockSpec((d, LANE), lambda i: (0, 0))],
        out_specs=pl.BlockSpec((tm, LANE), lambda i: (i, 0)),
        out_shape=jax.ShapeDtypeStruct((s, LANE), F32),
        compiler_params=_cparams(("parallel",), est),
        name="moe_router",
    )(x, g.reshape(1, d), wr)


def _row_gather(src_hbm, dst_ref, sem, n_rows, row_of):
    def issue(r, carry):
        pltpu.make_async_copy(src_hbm.at[pl.ds(row_of(r), 1)], dst_ref.at[pl.ds(r, 1)], sem).start()
        return carry
    lax.fori_loop(0, n_rows, issue, 0, unroll=DMA_ISSUE_UNROLL)
    pltpu.make_async_copy(src_hbm.at[pl.ds(0, n_rows)], dst_ref.at[pl.ds(0, n_rows)], sem).wait()


def _moe_kernel(tile_e_ref, nused_ref, tok_ref, x_hbm, gn_ref, wg_ref, wu_ref, wd_ref,
                o_ref, xg_ref, xb_ref, acc_ref, sem):
    i, f = pl.program_id(0), pl.program_id(1)
    tm = xg_ref.shape[0]
    used = i < nused_ref[0]
    last = f == pl.num_programs(1) - 1

    @pl.when(used & (f == 0))
    def _():
        _row_gather(x_hbm, xg_ref, sem, tm, lambda r: tok_ref[i * tm + r])
        xb_ref[...] = _rmsnorm_rows(xg_ref[...], gn_ref[...]).astype(BF16)
        acc_ref[...] = jnp.zeros(acc_ref.shape, F32)

    @pl.when(used)
    def _():
        hb = xb_ref[...]
        g = jnp.dot(hb, wg_ref[...], preferred_element_type=F32)
        u = jnp.dot(hb, wu_ref[...], preferred_element_type=F32)
        act = (g * _sigmoid(g) * u).astype(BF16)
        acc_ref[...] += jnp.dot(act, wd_ref[...], preferred_element_type=F32)

    @pl.when(used & last)
    def _():
        o_ref[...] = acc_ref[...]

    @pl.when(jnp.logical_not(used) & last)
    def _():
        o_ref[...] = jnp.zeros(o_ref.shape, F32)


def _moe_experts(x, gn, wg, wu, wd, tile_e, n_used, buf_tok):
    d = x.shape[1]
    fdim = wg.shape[2]
    cap = buf_tok.shape[0]
    tm, tf = TM_MOE, TF_MOE
    nf = fdim // tf

    def fsel(i, f, nused):
        return jnp.where(i < nused[0], f, nf - 1)

    est = (2 * (3 * _nbytes((d, tf), BF16) + _nbytes((tm, d), F32))
           + 2 * _nbytes((tm, d), F32) + _nbytes((tm, d), BF16)
           + 3 * _nbytes((tm, tf), F32) + _nbytes((tm, d), F32))
    grid_spec = pltpu.PrefetchScalarGridSpec(
        num_scalar_prefetch=3,
        grid=(cap // tm, nf),
        in_specs=[pl.BlockSpec(memory_space=pl.ANY),
                  pl.BlockSpec((1, d), lambda i, f, te, nu, tok: (0, 0)),
                  pl.BlockSpec((None, d, tf), lambda i, f, te, nu, tok: (te[i], 0, fsel(i, f, nu))),
                  pl.BlockSpec((None, d, tf), lambda i, f, te, nu, tok: (te[i], 0, fsel(i, f, nu))),
                  pl.BlockSpec((None, tf, d), lambda i, f, te, nu, tok: (te[i], fsel(i, f, nu), 0))],
        out_specs=pl.BlockSpec((tm, d), lambda i, f, te, nu, tok: (i, 0)),
        scratch_shapes=[pltpu.VMEM((tm, d), F32), pltpu.VMEM((tm, d), BF16),
                        pltpu.VMEM((tm, d), F32), pltpu.SemaphoreType.DMA(())])
    return pl.pallas_call(
        _moe_kernel,
        grid_spec=grid_spec,
        out_shape=jax.ShapeDtypeStruct((cap, d), F32),
        compiler_params=_cparams(("arbitrary", "arbitrary"), est),
        name="moe_experts",
    )(tile_e, n_used, buf_tok, x, gn.reshape(1, d), wg, wu, wd)


def _combine_kernel(pos_ref, x_ref, route_ref, y_hbm, gf_ref, o_ref, buf_ref, sem, *, final_norm):
    i = pl.program_id(0)
    tm = x_ref.shape[0]
    slot = i % 2

    def start_gather(step, dst_slot):
        def issue(r, carry):
            row = pos_ref[step * (TOP_K * tm) + r]
            pltpu.make_async_copy(y_hbm.at[pl.ds(row, 1)], buf_ref.at[dst_slot, pl.ds(r, 1)],
                                  sem.at[dst_slot]).start()
            return carry
        lax.fori_loop(0, TOP_K * tm, issue, 0, unroll=DMA_ISSUE_UNROLL)

    @pl.when(i == 0)
    def _():
        start_gather(0, 0)

    @pl.when(i + 1 < pl.num_programs(0))
    def _():
        start_gather(i + 1, 1 - slot)

    pltpu.make_async_copy(y_hbm.at[pl.ds(0, TOP_K * tm)], buf_ref.at[slot], sem.at[slot]).wait()
    route = route_ref[...]
    y = (route[:, TOP_K:TOP_K + 1] * buf_ref[slot, 0:tm, :]
         + route[:, TOP_K + 1:TOP_K + 2] * buf_ref[slot, tm:2 * tm, :])
    out = x_ref[...] + y
    if final_norm:
        out = _rmsnorm_rows(out, gf_ref[...])
    o_ref[...] = out


def _moe_combine(x, yb, pos, route, final_gain):
    s, d = x.shape
    tm = TM_COMB
    pos_flat = jnp.swapaxes(pos.reshape(s // tm, tm, TOP_K), 1, 2).reshape(-1)
    final_norm = final_gain is not None
    gf = final_gain if final_norm else jnp.ones((d,), F32)
    est = 2 * (2 * _nbytes((tm, d), F32) + _nbytes((tm, LANE), F32)) + 2 * _nbytes((TOP_K * tm, d), F32) \
        + 3 * _nbytes((tm, d), F32)
    grid_spec = pltpu.PrefetchScalarGridSpec(
        num_scalar_prefetch=1,
        grid=(s // tm,),
        in_specs=[pl.BlockSpec((tm, d), lambda i, pos: (i, 0)),
                  pl.BlockSpec((tm, LANE), lambda i, pos: (i, 0)),
                  pl.BlockSpec(memory_space=pl.ANY),
                  pl.BlockSpec((1, d), lambda i, pos: (0, 0))],
        out_specs=pl.BlockSpec((tm, d), lambda i, pos: (i, 0)),
        scratch_shapes=[pltpu.VMEM((2, TOP_K * tm, d), F32), pltpu.SemaphoreType.DMA((2,))])
    return pl.pallas_call(
        functools.partial(_combine_kernel, final_norm=final_norm),
        grid_spec=grid_spec,
        out_shape=jax.ShapeDtypeStruct((s, d), F32),
        compiler_params=_cparams(("arbitrary",), est),
        name="moe_combine",
    )(pos_flat, x, route, yb, gf.reshape(1, d))


def _moe_layer(x, gn, w_router, wg, wu, wd, final_gain=None):
    s = x.shape[0]
    tm = TM_MOE
    route = _router(x, gn, w_router)
    e_flat = route[:, 0:TOP_K].astype(jnp.int32).reshape(-1)
    n_assign = s * TOP_K
    onehot = (e_flat[:, None] == jnp.arange(N_EXPERTS, dtype=jnp.int32)[None, :]).astype(jnp.int32)
    csum = jnp.cumsum(onehot, axis=0)
    rank = jnp.sum(csum * onehot, axis=1) - 1
    counts = csum[-1]
    pcounts = (counts + tm - 1) // tm * tm
    pends = jnp.cumsum(pcounts)
    pstarts = pends - pcounts
    dest = jnp.sum(pstarts[None, :] * onehot, axis=1) + rank
    cap = n_assign + N_EXPERTS * tm
    tok_flat = jnp.arange(n_assign, dtype=jnp.int32) // TOP_K
    buf_tok = jnp.zeros((cap,), jnp.int32).at[dest].set(tok_flat)
    n_tiles = cap // tm
    tile_e = jnp.minimum(jnp.searchsorted(pends, jnp.arange(n_tiles, dtype=jnp.int32) * tm, side='right'),
                         N_EXPERTS - 1).astype(jnp.int32)
    n_used = (pends[-1:] // tm).astype(jnp.int32)
    yb = _moe_experts(x, gn, wg, wu, wd, tile_e, n_used, buf_tok)
    return _moe_combine(x, yb, dest.astype(jnp.int32).reshape(s, TOP_K), route, final_gain)


def kernel(x, w_in, w_branch_a, w_branch_b, w_branch_c, w_out, norm_mix, norm_ffn, norm_final, t5_bias, diff_lambda, diff_subln, qk_norm_b, na_rpb, ffn_gate, ffn_up, ffn_down, moe_router, moe_gate, moe_up, moe_down):
    bn, s, d = x.shape
    assert bn == 1 and d == D_MODEL
    depth = w_in.shape[0]
    xs = x.reshape(s, d)
    cos, sin_signed = _rope_tables(s)
    t5_flat = t5_bias.astype(F32).reshape(-1)
    colscale = jnp.ones((1, QKV_WIDTH), F32)
    colscale = colscale.at[:, SLAB_QA * LANE:SLAB_KA * LANE].set(A_QK_DIM ** -0.5 * LOG2E)
    colscale = colscale.at[:, SLAB_QC * LANE:SLAB_KC * LANE].set(HEAD_DIM ** -0.5 * LOG2E)

    for l in range(depth):
        h = _rmsnorm(xs, norm_mix[l], BF16)
        w_l = w_in[l].astype(BF16)
        z3 = _proj_slabs(h, w_l, colscale)
        gates = _matmul_sigmoid(h, w_l, QKV_WIDTH)

        lam_init = 0.8 - 0.6 * math.exp(-0.3 * l)
        lp = diff_lambda[l].astype(F32)
        lam = jnp.exp(jnp.sum(lp[0] * lp[1])) - jnp.exp(jnp.sum(lp[2] * lp[3])) + lam_init
        lam_pair = jnp.stack([lam, jnp.asarray(1.0 - lam_init, F32)]).astype(F32)
        vt_a = _transposed_v(z3[SLAB_VA:SLAB_VA + A_HEADS], TK_DIFF)
        vt_b = _transposed_v(z3[SLAB_VB:SLAB_VB + B_KV_HEADS], TK_GQA)
        ya = _diff_attention(z3, vt_a, t5_flat, lam_pair, diff_subln[l].astype(F32))

        q_gain = qk_norm_b[l, 0].astype(F32) * (HEAD_DIM ** -0.5 * LOG2E)
        k_gain = qk_norm_b[l, 1].astype(F32)
        gains = jnp.concatenate([jnp.tile(q_gain[None], (B_HEADS, 1)),
                                 jnp.tile(k_gain[None], (B_KV_HEADS, 1))])[:, None, :]
        yb = _gqa(_qk_prep(z3, gains, cos, sin_signed), vt_b)

        yc = _neighbourhood_attention(z3, na_rpb[l].astype(F32).reshape(-1))

        merged = _merge(ya, yb, yc, w_branch_a[l].astype(BF16), w_branch_b[l].astype(BF16),
                        w_branch_c[l].astype(BF16), gates)
        xs = _matmul_residual(merged, w_out[l].astype(BF16), xs, TM_OUT, TN_OUT, "out_proj")

        j = l // 2
        if l % 2 == 0:
            hf = _rmsnorm(xs, norm_ffn[l], BF16)
            act = _glu(hf, ffn_gate[j].astype(BF16), ffn_up[j].astype(BF16))
            xs = _matmul_residual(act, ffn_down[j].astype(BF16), xs, TM_DOWN, TN_DOWN, "ffn_down")
        else:
            last = l == depth - 1
            xs = _moe_layer(xs, norm_ffn[l], moe_router[j], moe_gate[j].astype(BF16),
                            moe_up[j].astype(BF16), moe_down[j].astype(BF16),
                            final_gain=norm_final.astype(F32) if last else None)
            if last:
                return xs.astype(x.dtype).reshape(bn, s, d)
    return _rmsnorm(xs, norm_final, x.dtype).reshape(bn, s, d)
```

```python
import functools
import math

import jax
import jax.numpy as jnp
from jax import lax
from jax.experimental import pallas as pl
from jax.experimental.pallas import tpu as pltpu

F32 = jnp.float32
BF16 = jnp.bfloat16

D_MODEL = 2048
GRID_W = 64
HEAD_DIM = 128
NORM_EPS = 1e-6
A_HEADS = 4
A_QK_DIM = 64
A_WIDTH = A_HEADS * HEAD_DIM
B_HEADS = 8
B_KV_HEADS = 2
B_GROUP = B_HEADS // B_KV_HEADS
B_WIDTH = B_HEADS * HEAD_DIM
C_HEADS = 4
C_WIDTH = C_HEADS * HEAD_DIM
NA_ROWS = 8
NA_COLS = 16
NA_ROFF = 2 * NA_ROWS - 1
NA_COFF = 2 * NA_COLS - 1
T5_BUCKETS = 32
T5_MAX_DIST = 128
ROPE_THETA = 10000.0
N_EXPERTS = 8
TOP_K = 2
QKV_WIDTH = 2 * A_WIDTH + A_WIDTH + B_WIDTH + 2 * B_KV_HEADS * HEAD_DIM + 3 * C_WIDTH
LOG2E = 1.4426950408889634
MASK_VALUE = -1e30

LANE = 128
V7X_VMEM_BYTES = 64 * 1024 * 1024
VMEM_CAP_BYTES = 56 * 1024 * 1024
QCHUNK = 256

SLAB_QA, SLAB_KA, SLAB_VA = 0, 4, 8
SLAB_QB, SLAB_KB, SLAB_VB = 12, 20, 22
SLAB_QC, SLAB_KC, SLAB_VC = 24, 28, 32
N_SLABS = QKV_WIDTH // LANE
VT_ROWS = HEAD_DIM + 16

TM_NORM = 512
TM_PROJ, TN_PROJ = 1024, 1536
TS_PREP = 4096
TQ_DIFF, TK_DIFF = 1024, 512
TQ_GQA, TK_GQA = 1024, 512
KV_UNROLL = 4
KV_UNROLL_DIFF = 8
QK_AHEAD = 3
BIAS_BLOCK = 128
NA_GROUP_ROWS = 8
NA_WIN_ROWS = 16
TM_MERGE, TN_MERGE = 512, 1024
TM_OUT, TN_OUT = 1024, 1024
TM_GLU, TF_GLU = 1024, 512
TM_DOWN, TN_DOWN = 1024, 512
TM_ROUTE = 512
TM_MOE, TF_MOE = 512, 1024
TM_COMB = 256
DMA_ISSUE_UNROLL = 8


def _cparams(semantics, vmem_bytes):
    return pltpu.CompilerParams(dimension_semantics=semantics,
                                vmem_limit_bytes=int(min(vmem_bytes, VMEM_CAP_BYTES)))


def _nbytes(shape, dtype):
    return math.prod(shape) * jnp.dtype(dtype).itemsize


def _rmsnorm_rows(x, g):
    ms = jnp.mean(x * x, axis=-1, keepdims=True)
    return x * lax.rsqrt(ms + NORM_EPS) * g


def _rmsnorm_kernel(x_ref, g_ref, o_ref):
    o_ref[...] = _rmsnorm_rows(x_ref[...], g_ref[...]).astype(o_ref.dtype)


def _rmsnorm(x, g, out_dtype):
    s, d = x.shape
    tm = TM_NORM
    est = 2 * (_nbytes((tm, d), F32) + _nbytes((tm, d), out_dtype)) + 3 * _nbytes((tm, d), F32)
    return pl.pallas_call(
        _rmsnorm_kernel,
        grid=(s // tm,),
        in_specs=[pl.BlockSpec((tm, d), lambda i: (i, 0)),
                  pl.BlockSpec((1, d), lambda i: (0, 0))],
        out_specs=pl.BlockSpec((tm, d), lambda i: (i, 0)),
        out_shape=jax.ShapeDtypeStruct((s, d), out_dtype),
        compiler_params=_cparams(("parallel",), est),
        name="rmsnorm",
    )(x, g.reshape(1, d))


def _proj_slab_kernel(a_ref, w_ref, cs_ref, o_ref):
    acc = jnp.dot(a_ref[...], w_ref[...], preferred_element_type=F32) * cs_ref[...]
    for c in range(o_ref.shape[0]):
        o_ref[c] = acc[:, c * LANE:(c + 1) * LANE].astype(o_ref.dtype)


def _proj_slabs(a, w, colscale):
    s, k = a.shape
    n = colscale.shape[1]
    tm, tn = TM_PROJ, TN_PROJ
    spt = tn // LANE
    est = (2 * (_nbytes((tm, k), BF16) + _nbytes((k, tn), BF16) + _nbytes((tm, tn), BF16))
           + 2 * _nbytes((tm, tn), F32))
    return pl.pallas_call(
        _proj_slab_kernel,
        grid=(s // tm, n // tn),
        in_specs=[pl.BlockSpec((tm, k), lambda i, j: (i, 0)),
                  pl.BlockSpec((k, tn), lambda i, j: (0, j)),
                  pl.BlockSpec((1, tn), lambda i, j: (0, j))],
        out_specs=pl.BlockSpec((spt, tm, LANE), lambda i, j: (j, i, 0)),
        out_shape=jax.ShapeDtypeStruct((n // LANE, s, LANE), BF16),
        compiler_params=_cparams(("parallel", "arbitrary"), est),
        name="qkv_proj",
    )(a, w, colscale)


def _sigmoid(x):
    return 1.0 / (1.0 + jnp.exp(-x))


def _matmul_sigmoid_kernel(a_ref, w_ref, o_ref):
    acc = jnp.dot(a_ref[...], w_ref[...], preferred_element_type=F32)
    o_ref[...] = _sigmoid(acc).astype(o_ref.dtype)


def _matmul_sigmoid(a, w, col0):
    s, k = a.shape
    tm, tn = TM_PROJ, TN_PROJ
    n = w.shape[1] - col0
    assert col0 % tn == 0 and n % tn == 0
    j0 = col0 // tn
    est = (2 * (_nbytes((tm, k), BF16) + _nbytes((k, tn), BF16) + _nbytes((tm, tn), BF16))
           + 2 * _nbytes((tm, tn), F32))
    return pl.pallas_call(
        _matmul_sigmoid_kernel,
        grid=(s // tm, n // tn),
        in_specs=[pl.BlockSpec((tm, k), lambda i, j: (i, 0)),
                  pl.BlockSpec((k, tn), lambda i, j: (0, j0 + j))],
        out_specs=pl.BlockSpec((tm, tn), lambda i, j: (i, j)),
        out_shape=jax.ShapeDtypeStruct((s, n), BF16),
        compiler_params=_cparams(("parallel", "arbitrary"), est),
        name="gate_proj",
    )(a, w)


def _matmul_residual_kernel(a_ref, w_ref, r_ref, o_ref):
    o_ref[...] = r_ref[...] + jnp.dot(a_ref[...], w_ref[...], preferred_element_type=F32)


def _matmul_residual(a, w, r, tm, tn, name):
    s, k = a.shape
    n = w.shape[1]
    est = (2 * (_nbytes((tm, k), BF16) + _nbytes((k, tn), BF16) + 2 * _nbytes((tm, tn), F32))
           + _nbytes((tm, tn), F32))
    return pl.pallas_call(
        _matmul_residual_kernel,
        grid=(s // tm, n // tn),
        in_specs=[pl.BlockSpec((tm, k), lambda i, j: (i, 0)),
                  pl.BlockSpec((k, tn), lambda i, j: (0, j)),
                  pl.BlockSpec((tm, tn), lambda i, j: (i, j))],
        out_specs=pl.BlockSpec((tm, tn), lambda i, j: (i, j)),
        out_shape=jax.ShapeDtypeStruct((s, n), F32),
        compiler_params=_cparams(("parallel", "arbitrary"), est),
        name=name,
    )(a, w, r)


def _merge_kernel(ya_ref, yb_ref, yc_ref, wa_ref, wb_ref, wc_ref, ga_ref, gb_ref, gc_ref, o_ref):
    acc = ga_ref[...].astype(F32) * jnp.dot(ya_ref[...], wa_ref[...], preferred_element_type=F32)
    acc += gb_ref[...].astype(F32) * jnp.dot(yb_ref[...], wb_ref[...], preferred_element_type=F32)
    acc += gc_ref[...].astype(F32) * jnp.dot(yc_ref[...], wc_ref[...], preferred_element_type=F32)
    o_ref[...] = acc.astype(o_ref.dtype)


def _merge(ya, yb, yc, wa, wb, wc, gates):
    s = ya.shape[0]
    d = wa.shape[1]
    tm, tn = TM_MERGE, TN_MERGE
    nj = d // tn
    kin = A_WIDTH + B_WIDTH + C_WIDTH
    est = (2 * (_nbytes((tm, kin), BF16) + _nbytes((kin, tn), BF16) + 4 * _nbytes((tm, tn), BF16))
           + 4 * _nbytes((tm, tn), F32))
    return pl.pallas_call(
        _merge_kernel,
        grid=(s // tm, nj),
        in_specs=[pl.BlockSpec((tm, A_WIDTH), lambda i, j: (i, 0)),
                  pl.BlockSpec((tm, B_WIDTH), lambda i, j: (i, 0)),
                  pl.BlockSpec((tm, C_WIDTH), lambda i, j: (i, 0)),
                  pl.BlockSpec((A_WIDTH, tn), lambda i, j: (0, j)),
                  pl.BlockSpec((B_WIDTH, tn), lambda i, j: (0, j)),
                  pl.BlockSpec((C_WIDTH, tn), lambda i, j: (0, j)),
                  pl.BlockSpec((tm, tn), lambda i, j: (i, j)),
                  pl.BlockSpec((tm, tn), lambda i, j: (i, nj + j)),
                  pl.BlockSpec((tm, tn), lambda i, j: (i, 2 * nj + j))],
        out_specs=pl.BlockSpec((tm, tn), lambda i, j: (i, j)),
        out_shape=jax.ShapeDtypeStruct((s, d), BF16),
        compiler_params=_cparams(("parallel", "arbitrary"), est),
        name="branch_merge",
    )(ya, yb, yc, wa, wb, wc, gates, gates, gates)


def _glu_kernel(h_ref, wg_ref, wu_ref, o_ref):
    h = h_ref[...]
    g = jnp.dot(h, wg_ref[...], preferred_element_type=F32)
    u = jnp.dot(h, wu_ref[...], preferred_element_type=F32)
    o_ref[...] = (g * _sigmoid(g) * u).astype(o_ref.dtype)


def _glu(h, wg, wu):
    s, k = h.shape
    f = wg.shape[1]
    tm, tf = TM_GLU, TF_GLU
    est = (2 * (_nbytes((tm, k), BF16) + 2 * _nbytes((k, tf), BF16) + _nbytes((tm, tf), BF16))
           + 3 * _nbytes((tm, tf), F32))
    return pl.pallas_call(
        _glu_kernel,
        grid=(s // tm, f // tf),
        in_specs=[pl.BlockSpec((tm, k), lambda i, j: (i, 0)),
                  pl.BlockSpec((k, tf), lambda i, j: (0, j)),
                  pl.BlockSpec((k, tf), lambda i, j: (0, j))],
        out_specs=pl.BlockSpec((tm, tf), lambda i, j: (i, j)),
        out_shape=jax.ShapeDtypeStruct((s, f), BF16),
        compiler_params=_cparams(("parallel", "arbitrary"), est),
        name="ffn_glu",
    )(h, wg, wu)


def _rope_tables(s):
    half = HEAD_DIM // 2
    pos = jnp.arange(s, dtype=jnp.int32)
    row = (pos // GRID_W).astype(F32)
    col = (pos % GRID_W).astype(F32)
    inv = ROPE_THETA ** (-jnp.arange(0, half, 2, dtype=F32) / half)
    ang_r = row[:, None] * inv[None, :]
    ang_c = col[:, None] * inv[None, :]
    ang = jnp.concatenate([ang_r, ang_r, ang_c, ang_c], axis=-1)
    sign = jnp.tile(jnp.concatenate([-jnp.ones((half // 2,), F32), jnp.ones((half // 2,), F32)]), 2)
    return jnp.cos(ang), jnp.sin(ang) * sign[None, :]


def _qk_prep_kernel(z_ref, g_ref, cos_ref, sin_ref, o_ref):
    y = _rmsnorm_rows(z_ref[...].astype(F32), g_ref[...])
    lane = lax.broadcasted_iota(jnp.int32, y.shape, 1)
    quarter = HEAD_DIM // 4
    first = (lane % (2 * quarter)) < quarter
    partner = jnp.where(first, pltpu.roll(y, HEAD_DIM - quarter, 1), pltpu.roll(y, quarter, 1))
    o_ref[...] = (y * cos_ref[...] + partner * sin_ref[...]).astype(o_ref.dtype)


def _qk_prep(z3, gains, cos, sin_signed):
    s = z3.shape[1]
    ts = TS_PREP
    nsl = B_HEADS + B_KV_HEADS
    est = 2 * (2 * _nbytes((ts, LANE), BF16) + 2 * _nbytes((ts, LANE), F32)) + 6 * _nbytes((ts, LANE), F32)
    return pl.pallas_call(
        _qk_prep_kernel,
        grid=(s // ts, nsl),
        in_specs=[pl.BlockSpec((None, ts, LANE), lambda i, n: (SLAB_QB + n, i, 0)),
                  pl.BlockSpec((None, 1, LANE), lambda i, n: (n, 0, 0)),
                  pl.BlockSpec((ts, LANE), lambda i, n: (i, 0)),
                  pl.BlockSpec((ts, LANE), lambda i, n: (i, 0))],
        out_specs=pl.BlockSpec((None, ts, LANE), lambda i, n: (n, i, 0)),
        out_shape=jax.ShapeDtypeStruct((nsl, s, LANE), BF16),
        compiler_params=_cparams(("parallel", "arbitrary"), est),
        name="gqa_qk_prep",
    )(z3, gains, cos, sin_signed)


def _softmax_update_t(k_tiles, vt_tiles, q_chunk, m_ref, acc_ref, bias_chunk=None, shift=None):
    n_chunks = m_ref.shape[1] // QCHUNK
    units = [(t, j) for t in range(len(k_tiles)) for j in range(n_chunks)]

    def scores(u):
        t, j = units[u]
        st = _qk_scores(k_tiles[t], q_chunk(j))
        return st if bias_chunk is None else st + bias_chunk(t, j)

    pending = [scores(u) for u in range(min(QK_AHEAD, len(units)))]
    for u, (t, j) in enumerate(units):
        cols = slice(j * QCHUNK, (j + 1) * QCHUNK)
        st = pending.pop(0)
        if u + QK_AHEAD < len(units):
            pending.append(scores(u + QK_AHEAD))
        m_prev = m_ref[:, cols]
        tile_max = jnp.max(st, axis=0, keepdims=True)
        if shift is None:
            m_new = jnp.maximum(m_prev, tile_max)
            sub = m_new
        else:
            m_new = jnp.maximum(m_prev, tile_max + shift)
            sub = m_new - shift
        alpha = jnp.exp2(m_prev - m_new)
        p = jnp.exp2(st - sub).astype(BF16)
        acc_ref[:, cols] = alpha * acc_ref[:, cols] + jnp.dot(vt_tiles[t], p, preferred_element_type=F32)
        m_ref[:, cols] = m_new


def _softmax_init(m_ref, acc_ref):
    m_ref[...] = jnp.full(m_ref.shape, -jnp.inf, F32)
    acc_ref[...] = jnp.zeros(acc_ref.shape, F32)


def _softmax_result_t(acc_ref):
    acc = acc_ref[...]
    return acc[0:HEAD_DIM] / acc[HEAD_DIM:HEAD_DIM + 1]


def _transposed_v(v, tk):
    n, s, _ = v.shape
    vt = jnp.swapaxes(v.reshape(n, s // tk, tk, HEAD_DIM), 2, 3)
    ones = jnp.ones((n, s // tk, VT_ROWS - HEAD_DIM, tk), BF16)
    return jnp.concatenate([vt, ones], axis=2)


def _qk_scores(a, b):
    return lax.dot_general(a, b, (((1,), (1,)), ((), ())), preferred_element_type=F32)


def _gqa_kernel(q_ref, k_ref, vt_ref, o_ref, m_ref, acc_ref):
    tq = q_ref.shape[1]
    n_k, _, tk = vt_ref.shape
    per_head = tq // QCHUNK

    def q_chunk(j):
        r0 = (j % per_head) * QCHUNK
        return q_ref[j // per_head, r0:r0 + QCHUNK, :]

    _softmax_init(m_ref, acc_ref)

    def key_tiles(kg, carry):
        tiles = [kg * KV_UNROLL + t for t in range(KV_UNROLL)]
        _softmax_update_t([k_ref[pl.ds(pl.multiple_of(ki * tk, tk), tk), :] for ki in tiles],
                          [vt_ref[ki] for ki in tiles], q_chunk, m_ref, acc_ref)
        return carry

    lax.fori_loop(0, n_k // KV_UNROLL, key_tiles, 0)
    ot = _softmax_result_t(acc_ref)
    for g in range(B_GROUP):
        o_ref[:, g * HEAD_DIM:(g + 1) * HEAD_DIM] = ot[:, g * tq:(g + 1) * tq].T.astype(o_ref.dtype)


def _gqa(qk, vt):
    s = qk.shape[1]
    tq = TQ_GQA
    n_k, _, tk = vt.shape[1:]
    assert n_k % KV_UNROLL == 0
    cols = B_GROUP * tq
    est = (2 * (2 * _nbytes((cols, LANE), BF16) + _nbytes((s, LANE), BF16) + _nbytes((VT_ROWS, s), BF16))
           + 3 * _nbytes((VT_ROWS, cols), F32) + 2 * (QK_AHEAD + 2) * _nbytes((tk, QCHUNK), F32))
    return pl.pallas_call(
        _gqa_kernel,
        grid=(B_KV_HEADS, s // tq),
        in_specs=[pl.BlockSpec((B_GROUP, tq, LANE), lambda h, i: (h, i, 0)),
                  pl.BlockSpec((None, s, LANE), lambda h, i: (B_HEADS + h, 0, 0)),
                  pl.BlockSpec((None, n_k, VT_ROWS, tk), lambda h, i: (h, 0, 0, 0))],
        out_specs=pl.BlockSpec((tq, B_GROUP * HEAD_DIM), lambda h, i: (i, h)),
        out_shape=jax.ShapeDtypeStruct((s, B_WIDTH), BF16),
        scratch_shapes=[pltpu.VMEM((1, cols), F32), pltpu.VMEM((VT_ROWS, cols), F32)],
        compiler_params=_cparams(("parallel", "parallel"), est),
        name="gqa_attention",
    )(qk, qk, vt)


def _t5_bias_block(t5_ref, h, rel0):
    shape = (BIAS_BLOCK, BIAS_BLOCK)
    rel = lax.broadcasted_iota(jnp.int32, shape, 0) - lax.broadcasted_iota(jnp.int32, shape, 1) + rel0
    nb = T5_BUCKETS // 2
    max_exact = nb // 2
    ret = jnp.where(rel > 0, nb, 0)
    n = jnp.abs(rel)
    nf = jnp.maximum(n, 1).astype(F32)
    large = max_exact + (jnp.log(nf / max_exact) / math.log(T5_MAX_DIST / max_exact)
                         * (nb - max_exact)).astype(jnp.int32)
    large = jnp.minimum(large, nb - 1)
    bucket = ret + jnp.where(n < max_exact, n, large)
    bias = jnp.full(shape, t5_ref[h], F32)
    for b in range(1, T5_BUCKETS):
        bias = jnp.where(bucket == b, t5_ref[b * A_HEADS + h], bias)
    return bias * LOG2E


def _t5_store_bias_tile(t5_ref, h, bias_ref, idx, key_offset, far_left, far_right):
    _, tk, tq = bias_ref.shape
    for rb in range(tk // BIAS_BLOCK):
        for cb in range(tq // BIAS_BLOCK):
            rel0 = key_offset + (rb - cb) * BIAS_BLOCK
            if rel0 - (BIAS_BLOCK - 1) >= T5_MAX_DIST:
                block = jnp.full((BIAS_BLOCK, BIAS_BLOCK), far_right, F32)
            elif rel0 + (BIAS_BLOCK - 1) <= -T5_MAX_DIST:
                block = jnp.full((BIAS_BLOCK, BIAS_BLOCK), far_left, F32)
            else:
                block = _t5_bias_block(t5_ref, h, rel0)
            bias_ref[idx, rb * BIAS_BLOCK:(rb + 1) * BIAS_BLOCK, cb * BIAS_BLOCK:(cb + 1) * BIAS_BLOCK] = block


def _diff_kernel(t5_ref, lam_ref, q_ref, k_ref, vt_ref, g_ref, o_ref, qs_ref, bias_ref, m_ref, acc_ref):
    h, qi = pl.program_id(0), pl.program_id(1)
    tq = q_ref.shape[0]
    n_k, _, tk = vt_ref.shape
    n_sub = tq // tk
    n_near = n_sub + 2
    per_map = tq // QCHUNK

    @pl.when(qi == 0)
    def _():
        far_left = t5_ref[(T5_BUCKETS // 2 - 1) * A_HEADS + h] * LOG2E
        far_right = t5_ref[(T5_BUCKETS - 1) * A_HEADS + h] * LOG2E
        bias_ref[0] = jnp.full((tk, tq), far_left, F32)
        for o in range(n_near):
            _t5_store_bias_tile(t5_ref, h, bias_ref, o + 1, (o - 1) * tk, far_left, far_right)
        bias_ref[n_near + 1] = jnp.full((tk, tq), far_right, F32)

    _softmax_init(m_ref, acc_ref)
    q = q_ref[...]
    lane = lax.broadcasted_iota(jnp.int32, q.shape, 1)
    zero = jnp.zeros_like(q)
    qs_ref[0:tq, :] = jnp.where(lane < A_QK_DIM, q, zero)
    qs_ref[tq:2 * tq, :] = jnp.where(lane >= A_QK_DIM, q, zero)

    def q_chunk(j):
        return qs_ref[j * QCHUNK:(j + 1) * QCHUNK, :]

    def tile_operands(kg):
        tiles = [kg * KV_UNROLL_DIFF + t for t in range(KV_UNROLL_DIFF)]
        return (tiles, [k_ref[pl.ds(pl.multiple_of(ki * tk, tk), tk), :] for ki in tiles],
                [vt_ref[ki] for ki in tiles])

    def near_tiles(kg, carry):
        tiles, k_tiles, vt_tiles = tile_operands(kg)
        bias_idx = [jnp.clip(ki - n_sub * qi + 2, 0, n_near + 1) for ki in tiles]

        def bias_chunk(t, j):
            c0 = (j % per_map) * QCHUNK
            return bias_ref[bias_idx[t], :, c0:c0 + QCHUNK]

        _softmax_update_t(k_tiles, vt_tiles, q_chunk, m_ref, acc_ref, bias_chunk=bias_chunk)
        return carry

    def far_tiles(kg, c):
        _, k_tiles, vt_tiles = tile_operands(kg)
        _softmax_update_t(k_tiles, vt_tiles, q_chunk, m_ref, acc_ref, shift=c)
        return c

    n_groups = n_k // KV_UNROLL_DIFF
    near_lo = jnp.maximum(n_sub * qi - 1, 0) // KV_UNROLL_DIFF
    near_hi = jnp.minimum((n_sub * qi + n_sub) // KV_UNROLL_DIFF + 1, n_groups)
    lax.fori_loop(0, near_lo, far_tiles, t5_ref[(T5_BUCKETS // 2 - 1) * A_HEADS + h] * LOG2E)
    lax.fori_loop(near_lo, near_hi, near_tiles, 0)
    lax.fori_loop(near_hi, n_groups, far_tiles, t5_ref[(T5_BUCKETS - 1) * A_HEADS + h] * LOG2E)
    ot = _softmax_result_t(acc_ref)
    out = (ot[:, 0:tq] - lam_ref[0] * ot[:, tq:2 * tq]).T
    o_ref[...] = (_rmsnorm_rows(out, g_ref[...]) * lam_ref[1]).astype(o_ref.dtype)


def _diff_attention(z3, vt, t5_flat, lam_pair, subln_g):
    s = z3.shape[1]
    tq = TQ_DIFF
    n_k, _, tk = vt.shape[1:]
    assert tk >= T5_MAX_DIST and tq % tk == 0 and s % tq == 0 and tq % QCHUNK == 0 and n_k % KV_UNROLL_DIFF == 0
    n_near = tq // tk + 2
    est = (2 * (2 * _nbytes((tq, LANE), BF16) + _nbytes((s, LANE), BF16) + _nbytes((VT_ROWS, s), BF16))
           + _nbytes((2 * tq, LANE), BF16) + (n_near + 2) * _nbytes((tk, tq), F32)
           + 3 * _nbytes((VT_ROWS, 2 * tq), F32) + 2 * (QK_AHEAD + 2) * _nbytes((tk, QCHUNK), F32))
    grid_spec = pltpu.PrefetchScalarGridSpec(
        num_scalar_prefetch=2,
        grid=(A_HEADS, s // tq),
        in_specs=[pl.BlockSpec((None, tq, LANE), lambda h, i, *_: (SLAB_QA + h, i, 0)),
                  pl.BlockSpec((None, s, LANE), lambda h, i, *_: (SLAB_KA + h, 0, 0)),
                  pl.BlockSpec((None, n_k, VT_ROWS, tk), lambda h, i, *_: (h, 0, 0, 0)),
                  pl.BlockSpec((1, LANE), lambda h, i, *_: (0, 0))],
        out_specs=pl.BlockSpec((tq, HEAD_DIM), lambda h, i, *_: (i, h)),
        scratch_shapes=[pltpu.VMEM((2 * tq, LANE), BF16), pltpu.VMEM((n_near + 2, tk, tq), F32),
                        pltpu.VMEM((1, 2 * tq), F32), pltpu.VMEM((VT_ROWS, 2 * tq), F32)])
    return pl.pallas_call(
        _diff_kernel,
        grid_spec=grid_spec,
        out_shape=jax.ShapeDtypeStruct((s, A_WIDTH), BF16),
        compiler_params=_cparams(("arbitrary", "arbitrary"), est),
        name="diff_attention",
    )(t5_flat, lam_pair, z3, z3, vt, subln_g.reshape(1, LANE))


def _na_window_start(first_row, rows):
    return max(0, min(first_row - NA_ROWS // 2, rows - NA_WIN_ROWS))


def _na_build_bias(rpb_ref, h, bias_ref, rows):
    cq = lax.broadcasted_iota(jnp.int32, (GRID_W, LANE), 0)
    ck = lax.broadcasted_iota(jnp.int32, (GRID_W, LANE), 1) % GRID_W
    cs = jnp.clip(cq - NA_COLS // 2, 0, GRID_W - NA_COLS)
    col_valid = (ck >= cs) & (ck < cs + NA_COLS)
    coff = jnp.clip(ck - cq + (NA_COLS - 1), 0, NA_COFF - 1)
    lower_half = lax.broadcasted_iota(jnp.int32, (GRID_W, LANE), 1) < GRID_W
    masked = jnp.full((GRID_W, LANE), MASK_VALUE, F32)
    col_bias = []
    for ro in range(NA_ROFF):
        base = (h * NA_ROFF + ro) * NA_COFF
        b = jnp.full((GRID_W, LANE), rpb_ref[base], F32)
        for c in range(1, NA_COFF):
            b = jnp.where(coff == c, rpb_ref[base + c], b)
        col_bias.append(jnp.where(col_valid, b * LOG2E, MASK_VALUE))
    n_groups = rows // NA_GROUP_ROWS
    for var, group in enumerate((0, 1, n_groups - 1)):
        ws = _na_window_start(group * NA_GROUP_ROWS, rows)
        for a in range(NA_GROUP_ROWS):
            r = group * NA_GROUP_ROWS + a
            rs = max(0, min(r - NA_ROWS // 2, rows - NA_ROWS))
            for ip in range(NA_WIN_ROWS // 2):
                halves = []
                for kr in (ws + 2 * ip, ws + 2 * ip + 1):
                    valid = rs <= kr < rs + NA_ROWS
                    halves.append(col_bias[kr - r + NA_ROWS - 1] if valid else masked)
                bias_ref[var, a * GRID_W:(a + 1) * GRID_W, ip * LANE:(ip + 1) * LANE] = (
                    jnp.where(lower_half, halves[0], halves[1]))


def _na_kernel(rpb_ref, q_ref, k_ref, v_ref, o_ref, bias_ref, *, rows):
    h, g = pl.program_id(0), pl.program_id(1)
    n_groups = rows // NA_GROUP_ROWS
    win = NA_WIN_ROWS * GRID_W

    @pl.when(g == 0)
    def _():
        _na_build_bias(rpb_ref, h, bias_ref, rows)

    ws = jnp.clip(g * NA_GROUP_ROWS - NA_ROWS // 2, 0, rows - NA_WIN_ROWS)
    start = pl.multiple_of(ws * GRID_W, (NA_ROWS // 2) * GRID_W)
    kw = k_ref[pl.ds(start, win), :]
    vw = v_ref[pl.ds(start, win), :]
    var = jnp.where(g == 0, 0, jnp.where(g == n_groups - 1, 2, 1))
    s = _qk_scores(q_ref[...], kw) + bias_ref[var]
    m = jnp.max(s, axis=-1, keepdims=True)
    p = jnp.exp2(s - m)
    l = jnp.sum(p, axis=-1, keepdims=True)
    o = jnp.dot(p.astype(BF16), vw, preferred_element_type=F32) / l
    o_ref[...] = o.astype(o_ref.dtype)


def _neighbourhood_attention(z3, rpb_flat):
    s = z3.shape[1]
    rows = s // GRID_W
    assert rows % NA_GROUP_ROWS == 0 and rows >= 3 * NA_GROUP_ROWS
    tq = NA_GROUP_ROWS * GRID_W
    win = NA_WIN_ROWS * GRID_W
    est = (2 * (2 * _nbytes((tq, LANE), BF16) + 2 * _nbytes((s, LANE), BF16))
           + 3 * _nbytes((tq, win), F32) + 3 * _nbytes((tq, win), F32))
    grid_spec = pltpu.PrefetchScalarGridSpec(
        num_scalar_prefetch=1,
        grid=(C_HEADS, rows // NA_GROUP_ROWS),
        in_specs=[pl.BlockSpec((None, tq, LANE), lambda h, g, *_: (SLAB_QC + h, g, 0)),
                  pl.BlockSpec((None, s, LANE), lambda h, g, *_: (SLAB_KC + h, 0, 0)),
                  pl.BlockSpec((None, s, LANE), lambda h, g, *_: (SLAB_VC + h, 0, 0))],
        out_specs=pl.BlockSpec((tq, HEAD_DIM), lambda h, g, *_: (g, h)),
        scratch_shapes=[pltpu.VMEM((3, tq, win), F32)])
    return pl.pallas_call(
        functools.partial(_na_kernel, rows=rows),
        grid_spec=grid_spec,
        out_shape=jax.ShapeDtypeStruct((s, C_WIDTH), BF16),
        compiler_params=_cparams(("arbitrary", "arbitrary"), est),
        name="neighbourhood_attention",
    )(rpb_flat, z3, z3, z3)


def _router_kernel(x_ref, g_ref, wr_ref, o_ref):
    h = _rmsnorm_rows(x_ref[...], g_ref[...])
    logits = jnp.dot(h, wr_ref[...], preferred_element_type=F32, precision=lax.Precision.HIGHEST)
    lane = lax.broadcasted_iota(jnp.int32, logits.shape, 1)
    lane_f = lane.astype(F32)
    logits = jnp.where(lane < N_EXPERTS, logits, -jnp.inf)
    m1 = jnp.max(logits, axis=-1, keepdims=True)
    i1 = jnp.min(jnp.where(logits == m1, lane_f, float(LANE)), axis=-1, keepdims=True)
    rest = jnp.where(lane_f == i1, -jnp.inf, logits)
    m2 = jnp.max(rest, axis=-1, keepdims=True)
    i2 = jnp.min(jnp.where(rest == m2, lane_f, float(LANE)), axis=-1, keepdims=True)
    e = jnp.exp(m2 - m1)
    g1 = 1.0 / (1.0 + e)
    g2 = e / (1.0 + e)
    o_ref[...] = jnp.where(lane == 0, i1, jnp.where(lane == 1, i2, jnp.where(lane == 2, g1, g2)))


def _router(x, g, w_router):
    s, d = x.shape
    tm = TM_ROUTE
    wr = jnp.zeros((d, LANE), F32).at[:, :N_EXPERTS].set(w_router)
    est = 2 * (_nbytes((tm, d), F32) + _nbytes((d, LANE), F32) + _nbytes((tm, LANE), F32)) + 4 * _nbytes((tm, d), F32)
    return pl.pallas_call(
        _router_kernel,
        grid=(s // tm,),
        in_specs=[pl.BlockSpec((tm, d), lambda i: (i, 0)),
                  pl.BlockSpec((1, d), lambda i: (0, 0)),
                  pl.BlockSpec((d, LANE), lambda i: (0, 0))],
        out_specs=pl.BlockSpec((tm, LANE), lambda i: (i, 0)),
        out_shape=jax.ShapeDtypeStruct((s, LANE), F32),
        compiler_params=_cparams(("parallel",), est),
        name="moe_router",
    )(x, g.reshape(1, d), wr)


def _row_gather(src_hbm, dst_ref, sem, n_rows, row_of):
    def issue(r, carry):
        pltpu.make_async_copy(src_hbm.at[pl.ds(row_of(r), 1)], dst_ref.at[pl.ds(r, 1)], sem).start()
        return carry
    lax.fori_loop(0, n_rows, issue, 0, unroll=DMA_ISSUE_UNROLL)
    pltpu.make_async_copy(src_hbm.at[pl.ds(0, n_rows)], dst_ref.at[pl.ds(0, n_rows)], sem).wait()


def _moe_kernel(tile_e_ref, nused_ref, tok_ref, x_hbm, gn_ref, wg_ref, wu_ref, wd_ref,
                o_ref, xg_ref, xb_ref, acc_ref, sem):
    i, f = pl.program_id(0), pl.program_id(1)
    tm = xg_ref.shape[0]
    used = i < nused_ref[0]
    last = f == pl.num_programs(1) - 1

    @pl.when(used & (f == 0))
    def _():
        _row_gather(x_hbm, xg_ref, sem, tm, lambda r: tok_ref[i * tm + r])
        xb_ref[...] = _rmsnorm_rows(xg_ref[...], gn_ref[...]).astype(BF16)
        acc_ref[...] = jnp.zeros(acc_ref.shape, F32)

    @pl.when(used)
    def _():
        hb = xb_ref[...]
        g = jnp.dot(hb, wg_ref[...], preferred_element_type=F32)
        u = jnp.dot(hb, wu_ref[...], preferred_element_type=F32)
        act = (g * _sigmoid(g) * u).astype(BF16)
        acc_ref[...] += jnp.dot(act, wd_ref[...], preferred_element_type=F32)

    @pl.when(used & last)
    def _():
        o_ref[...] = acc_ref[...]

    @pl.when(jnp.logical_not(used) & last)
    def _():
        o_ref[...] = jnp.zeros(o_ref.shape, F32)


def _moe_experts(x, gn, wg, wu, wd, tile_e, n_used, buf_tok):
    d = x.shape[1]
    fdim = wg.shape[2]
    cap = buf_tok.shape[0]
    tm, tf = TM_MOE, TF_MOE
    nf = fdim // tf

    def fsel(i, f, nused):
        return jnp.where(i < nused[0], f, nf - 1)

    est = (2 * (3 * _nbytes((d, tf), BF16) + _nbytes((tm, d), F32))
           + 2 * _nbytes((tm, d), F32) + _nbytes((tm, d), BF16)
           + 3 * _nbytes((tm, tf), F32) + _nbytes((tm, d), F32))
    grid_spec = pltpu.PrefetchScalarGridSpec(
        num_scalar_prefetch=3,
        grid=(cap // tm, nf),
        in_specs=[pl.BlockSpec(memory_space=pl.ANY),
                  pl.BlockSpec((1, d), lambda i, f, te, nu, tok: (0, 0)),
                  pl.BlockSpec((None, d, tf), lambda i, f, te, nu, tok: (te[i], 0, fsel(i, f, nu))),
                  pl.BlockSpec((None, d, tf), lambda i, f, te, nu, tok: (te[i], 0, fsel(i, f, nu))),
                  pl.BlockSpec((None, tf, d), lambda i, f, te, nu, tok: (te[i], fsel(i, f, nu), 0))],
        out_specs=pl.BlockSpec((tm, d), lambda i, f, te, nu, tok: (i, 0)),
        scratch_shapes=[pltpu.VMEM((tm, d), F32), pltpu.VMEM((tm, d), BF16),
                        pltpu.VMEM((tm, d), F32), pltpu.SemaphoreType.DMA(())])
    return pl.pallas_call(
        _moe_kernel,
        grid_spec=grid_spec,
        out_shape=jax.ShapeDtypeStruct((cap, d), F32),
        compiler_params=_cparams(("arbitrary", "arbitrary"), est),
        name="moe_experts",
    )(tile_e, n_used, buf_tok, x, gn.reshape(1, d), wg, wu, wd)


def _combine_kernel(pos_ref, x_ref, route_ref, y_hbm, gf_ref, o_ref, buf_ref, sem, *, final_norm):
    i = pl.program_id(0)
    tm = x_ref.shape[0]
    slot = i % 2

    def start_gather(step, dst_slot):
        def issue(r, carry):
            row = pos_ref[step * (TOP_K * tm) + r]
            pltpu.make_async_copy(y_hbm.at[pl.ds(row, 1)], buf_ref.at[dst_slot, pl.ds(r, 1)],
                                  sem.at[dst_slot]).start()
            return carry
        lax.fori_loop(0, TOP_K * tm, issue, 0, unroll=DMA_ISSUE_UNROLL)

    @pl.when(i == 0)
    def _():
        start_gather(0, 0)

    @pl.when(i + 1 < pl.num_programs(0))
    def _():
        start_gather(i + 1, 1 - slot)

    pltpu.make_async_copy(y_hbm.at[pl.ds(0, TOP_K * tm)], buf_ref.at[slot], sem.at[slot]).wait()
    route = route_ref[...]
    y = (route[:, TOP_K:TOP_K + 1] * buf_ref[slot, 0:tm, :]
         + route[:, TOP_K + 1:TOP_K + 2] * buf_ref[slot, tm:2 * tm, :])
    out = x_ref[...] + y
    if final_norm:
        out = _rmsnorm_rows(out, gf_ref[...])
    o_ref[...] = out


def _moe_combine(x, yb, pos, route, final_gain):
    s, d = x.shape
    tm = TM_COMB
    pos_flat = jnp.swapaxes(pos.reshape(s // tm, tm, TOP_K), 1, 2).reshape(-1)
    final_norm = final_gain is not None
    gf = final_gain if final_norm else jnp.ones((d,), F32)
    est = 2 * (2 * _nbytes((tm, d), F32) + _nbytes((tm, LANE), F32)) + 2 * _nbytes((TOP_K * tm, d), F32) \
        + 3 * _nbytes((tm, d), F32)
    grid_spec = pltpu.PrefetchScalarGridSpec(
        num_scalar_prefetch=1,
        grid=(s // tm,),
        in_specs=[pl.BlockSpec((tm, d), lambda i, pos: (i, 0)),
                  pl.BlockSpec((tm, LANE), lambda i, pos: (i, 0)),
                  pl.BlockSpec(memory_space=pl.ANY),
                  pl.BlockSpec((1, d), lambda i, pos: (0, 0))],
        out_specs=pl.BlockSpec((tm, d), lambda i, pos: (i, 0)),
        scratch_shapes=[pltpu.VMEM((2, TOP_K * tm, d), F32), pltpu.SemaphoreType.DMA((2,))])
    return pl.pallas_call(
        functools.partial(_combine_kernel, final_norm=final_norm),
        grid_spec=grid_spec,
        out_shape=jax.ShapeDtypeStruct((s, d), F32),
        compiler_params=_cparams(("arbitrary",), est),
        name="moe_combine",
    )(pos_flat, x, route, yb, gf.reshape(1, d))


def _moe_layer(x, gn, w_router, wg, wu, wd, final_gain=None):
    s = x.shape[0]
    tm = TM_MOE
    route = _router(x, gn, w_router)
    e_flat = route[:, 0:TOP_K].astype(jnp.int32).reshape(-1)
    n_assign = s * TOP_K
    onehot = (e_flat[:, None] == jnp.arange(N_EXPERTS, dtype=jnp.int32)[None, :]).astype(jnp.int32)
    csum = jnp.cumsum(onehot, axis=0)
    rank = jnp.sum(csum * onehot, axis=1) - 1
    counts = csum[-1]
    pcounts = (counts + tm - 1) // tm * tm
    pends = jnp.cumsum(pcounts)
    pstarts = pends - pcounts
    dest = jnp.sum(pstarts[None, :] * onehot, axis=1) + rank
    cap = n_assign + N_EXPERTS * tm
    tok_flat = jnp.arange(n_assign, dtype=jnp.int32) // TOP_K
    buf_tok = jnp.zeros((cap,), jnp.int32).at[dest].set(tok_flat)
    n_tiles = cap // tm
    tile_e = jnp.minimum(jnp.searchsorted(pends, jnp.arange(n_tiles, dtype=jnp.int32) * tm, side='right'),
                         N_EXPERTS - 1).astype(jnp.int32)
    n_used = (pends[-1:] // tm).astype(jnp.int32)
    yb = _moe_experts(x, gn, wg, wu, wd, tile_e, n_used, buf_tok)
    return _moe_combine(x, yb, dest.astype(jnp.int32).reshape(s, TOP_K), route, final_gain)


def kernel(x, w_in, w_branch_a, w_branch_b, w_branch_c, w_out, norm_mix, norm_ffn, norm_final, t5_bias, diff_lambda, diff_subln, qk_norm_b, na_rpb, ffn_gate, ffn_up, ffn_down, moe_router, moe_gate, moe_up, moe_down):
    bn, s, d = x.shape
    assert bn == 1 and d == D_MODEL
    depth = w_in.shape[0]
    xs = x.reshape(s, d)
    cos, sin_signed = _rope_tables(s)
    t5_flat = t5_bias.astype(F32).reshape(-1)
    colscale = jnp.ones((1, QKV_WIDTH), F32)
    colscale = colscale.at[:, SLAB_QA * LANE:SLAB_KA * LANE].set(A_QK_DIM ** -0.5 * LOG2E)
    colscale = colscale.at[:, SLAB_QC * LANE:SLAB_KC * LANE].set(HEAD_DIM ** -0.5 * LOG2E)

    for l in range(depth):
        h = _rmsnorm(xs, norm_mix[l], BF16)
        w_l = w_in[l].astype(BF16)
        z3 = _proj_slabs(h, w_l, colscale)
        gates = _matmul_sigmoid(h, w_l, QKV_WIDTH)

        lam_init = 0.8 - 0.6 * math.exp(-0.3 * l)
        lp = diff_lambda[l].astype(F32)
        lam = jnp.exp(jnp.sum(lp[0] * lp[1])) - jnp.exp(jnp.sum(lp[2] * lp[3])) + lam_init
        lam_pair = jnp.stack([lam, jnp.asarray(1.0 - lam_init, F32)]).astype(F32)
        vt_a = _transposed_v(z3[SLAB_VA:SLAB_VA + A_HEADS], TK_DIFF)
        vt_b = _transposed_v(z3[SLAB_VB:SLAB_VB + B_KV_HEADS], TK_GQA)
        ya = _diff_attention(z3, vt_a, t5_flat, lam_pair, diff_subln[l].astype(F32))

        q_gain = qk_norm_b[l, 0].astype(F32) * (HEAD_DIM ** -0.5 * LOG2E)
        k_gain = qk_norm_b[l, 1].astype(F32)
        gains = jnp.concatenate([jnp.tile(q_gain[None], (B_HEADS, 1)),
                                 jnp.tile(k_gain[None], (B_KV_HEADS, 1))])[:, None, :]
        yb = _gqa(_qk_prep(z3, gains, cos, sin_signed), vt_b)

        yc = _neighbourhood_attention(z3, na_rpb[l].astype(F32).reshape(-1))

        merged = _merge(ya, yb, yc, w_branch_a[l].astype(BF16), w_branch_b[l].astype(BF16),
                        w_branch_c[l].astype(BF16), gates)
        xs = _matmul_residual(merged, w_out[l].astype(BF16), xs, TM_OUT, TN_OUT, "out_proj")

        j = l // 2
        if l % 2 == 0:
            hf = _rmsnorm(xs, norm_ffn[l], BF16)
            act = _glu(hf, ffn_gate[j].astype(BF16), ffn_up[j].astype(BF16))
            xs = _matmul_residual(act, ffn_down[j].astype(BF16), xs, TM_DOWN, TN_DOWN, "ffn_down")
        else:
            last = l == depth - 1
            xs = _moe_layer(xs, norm_ffn[l], moe_router[j], moe_gate[j].astype(BF16),
                            moe_up[j].astype(BF16), moe_down[j].astype(BF16),
                            final_gain=norm_final.astype(F32) if last else None)
            if last:
                return xs.astype(x.dtype).reshape(bn, s, d)
    return _rmsnorm(xs, norm_final, x.dtype).reshape(bn, s, d)
```

```python
import functools
import math

import jax
import jax.numpy as jnp
from jax import lax
from jax.experimental import pallas as pl
from jax.experimental.pallas import tpu as pltpu

F32 = jnp.float32
BF16 = jnp.bfloat16

D_MODEL = 2048
GRID_W = 64
HEAD_DIM = 128
NORM_EPS = 1e-6
A_HEADS = 4
A_QK_DIM = 64
A_WIDTH = A_HEADS * HEAD_DIM
B_HEADS = 8
B_KV_HEADS = 2
B_GROUP = B_HEADS // B_KV_HEADS
B_WIDTH = B_HEADS * HEAD_DIM
C_HEADS = 4
C_WIDTH = C_HEADS * HEAD_DIM
NA_ROWS = 8
NA_COLS = 16
NA_ROFF = 2 * NA_ROWS - 1
NA_COFF = 2 * NA_COLS - 1
T5_BUCKETS = 32
T5_MAX_DIST = 128
ROPE_THETA = 10000.0
N_EXPERTS = 8
TOP_K = 2
QKV_WIDTH = 2 * A_WIDTH + A_WIDTH + B_WIDTH + 2 * B_KV_HEADS * HEAD_DIM + 3 * C_WIDTH
LOG2E = 1.4426950408889634
MASK_VALUE = -1e30

LANE = 128
V7X_VMEM_BYTES = 64 * 1024 * 1024
VMEM_CAP_BYTES = 56 * 1024 * 1024
QCHUNK = 256

SLAB_QA, SLAB_KA, SLAB_VA = 0, 4, 8
SLAB_QB, SLAB_KB, SLAB_VB = 12, 20, 22
SLAB_QC, SLAB_KC, SLAB_VC = 24, 28, 32
N_SLABS = QKV_WIDTH // LANE
VT_ROWS = HEAD_DIM + 16

TM_NORM = 512
TM_PROJ, TN_PROJ = 1024, 1536
TS_PREP = 4096
TQ_DIFF, TK_DIFF = 1024, 512
TQ_GQA, TK_GQA = 2048, 512
KV_UNROLL = 4
KV_UNROLL_DIFF = 8
QK_AHEAD = 3
BIAS_BLOCK = 128
NA_GROUP_ROWS = 8
NA_WIN_ROWS = 16
TM_MERGE, TN_MERGE = 512, 1024
TM_OUT, TN_OUT = 1024, 1024
TM_GLU, TF_GLU = 1024, 512
TM_DOWN, TN_DOWN = 1024, 512
TM_ROUTE = 512
TM_MOE, TF_MOE = 512, 1024
TM_COMB = 256
DMA_ISSUE_UNROLL = 8


def _cparams(semantics, vmem_bytes):
    return pltpu.CompilerParams(dimension_semantics=semantics,
                                vmem_limit_bytes=int(min(vmem_bytes, VMEM_CAP_BYTES)))


def _nbytes(shape, dtype):
    return math.prod(shape) * jnp.dtype(dtype).itemsize


def _rmsnorm_rows(x, g):
    ms = jnp.mean(x * x, axis=-1, keepdims=True)
    return x * lax.rsqrt(ms + NORM_EPS) * g


def _rmsnorm_kernel(x_ref, g_ref, o_ref):
    o_ref[...] = _rmsnorm_rows(x_ref[...], g_ref[...]).astype(o_ref.dtype)


def _rmsnorm(x, g, out_dtype):
    s, d = x.shape
    tm = TM_NORM
    est = 2 * (_nbytes((tm, d), F32) + _nbytes((tm, d), out_dtype)) + 3 * _nbytes((tm, d), F32)
    return pl.pallas_call(
        _rmsnorm_kernel,
        grid=(s // tm,),
        in_specs=[pl.BlockSpec((tm, d), lambda i: (i, 0)),
                  pl.BlockSpec((1, d), lambda i: (0, 0))],
        out_specs=pl.BlockSpec((tm, d), lambda i: (i, 0)),
        out_shape=jax.ShapeDtypeStruct((s, d), out_dtype),
        compiler_params=_cparams(("parallel",), est),
        name="rmsnorm",
    )(x, g.reshape(1, d))


def _proj_slab_kernel(a_ref, w_ref, cs_ref, o_ref):
    acc = jnp.dot(a_ref[...], w_ref[...], preferred_element_type=F32) * cs_ref[...]
    for c in range(o_ref.shape[0]):
        o_ref[c] = acc[:, c * LANE:(c + 1) * LANE].astype(o_ref.dtype)


def _proj_slabs(a, w, colscale):
    s, k = a.shape
    n = colscale.shape[1]
    tm, tn = TM_PROJ, TN_PROJ
    spt = tn // LANE
    est = (2 * (_nbytes((tm, k), BF16) + _nbytes((k, tn), BF16) + _nbytes((tm, tn), BF16))
           + 2 * _nbytes((tm, tn), F32))
    return pl.pallas_call(
        _proj_slab_kernel,
        grid=(s // tm, n // tn),
        in_specs=[pl.BlockSpec((tm, k), lambda i, j: (i, 0)),
                  pl.BlockSpec((k, tn), lambda i, j: (0, j)),
                  pl.BlockSpec((1, tn), lambda i, j: (0, j))],
        out_specs=pl.BlockSpec((spt, tm, LANE), lambda i, j: (j, i, 0)),
        out_shape=jax.ShapeDtypeStruct((n // LANE, s, LANE), BF16),
        compiler_params=_cparams(("parallel", "arbitrary"), est),
        name="qkv_proj",
    )(a, w, colscale)


def _sigmoid(x):
    return 1.0 / (1.0 + jnp.exp(-x))


def _matmul_sigmoid_kernel(a_ref, w_ref, o_ref):
    acc = jnp.dot(a_ref[...], w_ref[...], preferred_element_type=F32)
    o_ref[...] = _sigmoid(acc).astype(o_ref.dtype)


def _matmul_sigmoid(a, w, col0):
    s, k = a.shape
    tm, tn = TM_PROJ, TN_PROJ
    n = w.shape[1] - col0
    assert col0 % tn == 0 and n % tn == 0
    j0 = col0 // tn
    est = (2 * (_nbytes((tm, k), BF16) + _nbytes((k, tn), BF16) + _nbytes((tm, tn), BF16))
           + 2 * _nbytes((tm, tn), F32))
    return pl.pallas_call(
        _matmul_sigmoid_kernel,
        grid=(s // tm, n // tn),
        in_specs=[pl.BlockSpec((tm, k), lambda i, j: (i, 0)),
                  pl.BlockSpec((k, tn), lambda i, j: (0, j0 + j))],
        out_specs=pl.BlockSpec((tm, tn), lambda i, j: (i, j)),
        out_shape=jax.ShapeDtypeStruct((s, n), BF16),
        compiler_params=_cparams(("parallel", "arbitrary"), est),
        name="gate_proj",
    )(a, w)


def _matmul_residual_kernel(a_ref, w_ref, r_ref, o_ref):
    o_ref[...] = r_ref[...] + jnp.dot(a_ref[...], w_ref[...], preferred_element_type=F32)


def _matmul_residual(a, w, r, tm, tn, name):
    s, k = a.shape
    n = w.shape[1]
    est = (2 * (_nbytes((tm, k), BF16) + _nbytes((k, tn), BF16) + 2 * _nbytes((tm, tn), F32))
           + _nbytes((tm, tn), F32))
    return pl.pallas_call(
        _matmul_residual_kernel,
        grid=(s // tm, n // tn),
        in_specs=[pl.BlockSpec((tm, k), lambda i, j: (i, 0)),
                  pl.BlockSpec((k, tn), lambda i, j: (0, j)),
                  pl.BlockSpec((tm, tn), lambda i, j: (i, j))],
        out_specs=pl.BlockSpec((tm, tn), lambda i, j: (i, j)),
        out_shape=jax.ShapeDtypeStruct((s, n), F32),
        compiler_params=_cparams(("parallel", "arbitrary"), est),
        name=name,
    )(a, w, r)


def _merge_kernel(ya_ref, yb_ref, yc_ref, wa_ref, wb_ref, wc_ref, ga_ref, gb_ref, gc_ref, o_ref):
    acc = ga_ref[...].astype(F32) * jnp.dot(ya_ref[...], wa_ref[...], preferred_element_type=F32)
    acc += gb_ref[...].astype(F32) * jnp.dot(yb_ref[...], wb_ref[...], preferred_element_type=F32)
    acc += gc_ref[...].astype(F32) * jnp.dot(yc_ref[...], wc_ref[...], preferred_element_type=F32)
    o_ref[...] = acc.astype(o_ref.dtype)


def _merge(ya, yb, yc, wa, wb, wc, gates):
    s = ya.shape[0]
    d = wa.shape[1]
    tm, tn = TM_MERGE, TN_MERGE
    nj = d // tn
    kin = A_WIDTH + B_WIDTH + C_WIDTH
    est = (2 * (_nbytes((tm, kin), BF16) + _nbytes((kin, tn), BF16) + 4 * _nbytes((tm, tn), BF16))
           + 4 * _nbytes((tm, tn), F32))
    return pl.pallas_call(
        _merge_kernel,
        grid=(s // tm, nj),
        in_specs=[pl.BlockSpec((tm, A_WIDTH), lambda i, j: (i, 0)),
                  pl.BlockSpec((tm, B_WIDTH), lambda i, j: (i, 0)),
                  pl.BlockSpec((tm, C_WIDTH), lambda i, j: (i, 0)),
                  pl.BlockSpec((A_WIDTH, tn), lambda i, j: (0, j)),
                  pl.BlockSpec((B_WIDTH, tn), lambda i, j: (0, j)),
                  pl.BlockSpec((C_WIDTH, tn), lambda i, j: (0, j)),
                  pl.BlockSpec((tm, tn), lambda i, j: (i, j)),
                  pl.BlockSpec((tm, tn), lambda i, j: (i, nj + j)),
                  pl.BlockSpec((tm, tn), lambda i, j: (i, 2 * nj + j))],
        out_specs=pl.BlockSpec((tm, tn), lambda i, j: (i, j)),
        out_shape=jax.ShapeDtypeStruct((s, d), BF16),
        compiler_params=_cparams(("parallel", "arbitrary"), est),
        name="branch_merge",
    )(ya, yb, yc, wa, wb, wc, gates, gates, gates)


def _glu_kernel(h_ref, wg_ref, wu_ref, o_ref):
    h = h_ref[...]
    g = jnp.dot(h, wg_ref[...], preferred_element_type=F32)
    u = jnp.dot(h, wu_ref[...], preferred_element_type=F32)
    o_ref[...] = (g * _sigmoid(g) * u).astype(o_ref.dtype)


def _glu(h, wg, wu):
    s, k = h.shape
    f = wg.shape[1]
    tm, tf = TM_GLU, TF_GLU
    est = (2 * (_nbytes((tm, k), BF16) + 2 * _nbytes((k, tf), BF16) + _nbytes((tm, tf), BF16))
           + 3 * _nbytes((tm, tf), F32))
    return pl.pallas_call(
        _glu_kernel,
        grid=(s // tm, f // tf),
        in_specs=[pl.BlockSpec((tm, k), lambda i, j: (i, 0)),
                  pl.BlockSpec((k, tf), lambda i, j: (0, j)),
                  pl.BlockSpec((k, tf), lambda i, j: (0, j))],
        out_specs=pl.BlockSpec((tm, tf), lambda i, j: (i, j)),
        out_shape=jax.ShapeDtypeStruct((s, f), BF16),
        compiler_params=_cparams(("parallel", "arbitrary"), est),
        name="ffn_glu",
    )(h, wg, wu)


def _rope_tables(s):
    half = HEAD_DIM // 2
    pos = jnp.arange(s, dtype=jnp.int32)
    row = (pos // GRID_W).astype(F32)
    col = (pos % GRID_W).astype(F32)
    inv = ROPE_THETA ** (-jnp.arange(0, half, 2, dtype=F32) / half)
    ang_r = row[:, None] * inv[None, :]
    ang_c = col[:, None] * inv[None, :]
    ang = jnp.concatenate([ang_r, ang_r, ang_c, ang_c], axis=-1)
    sign = jnp.tile(jnp.concatenate([-jnp.ones((half // 2,), F32), jnp.ones((half // 2,), F32)]), 2)
    return jnp.cos(ang), jnp.sin(ang) * sign[None, :]


def _qk_prep_kernel(z_ref, g_ref, cos_ref, sin_ref, o_ref):
    y = _rmsnorm_rows(z_ref[...].astype(F32), g_ref[...])
    lane = lax.broadcasted_iota(jnp.int32, y.shape, 1)
    quarter = HEAD_DIM // 4
    first = (lane % (2 * quarter)) < quarter
    partner = jnp.where(first, pltpu.roll(y, HEAD_DIM - quarter, 1), pltpu.roll(y, quarter, 1))
    o_ref[...] = (y * cos_ref[...] + partner * sin_ref[...]).astype(o_ref.dtype)


def _qk_prep(z3, gains, cos, sin_signed):
    s = z3.shape[1]
    ts = TS_PREP
    nsl = B_HEADS + B_KV_HEADS
    est = 2 * (2 * _nbytes((ts, LANE), BF16) + 2 * _nbytes((ts, LANE), F32)) + 6 * _nbytes((ts, LANE), F32)
    return pl.pallas_call(
        _qk_prep_kernel,
        grid=(s // ts, nsl),
        in_specs=[pl.BlockSpec((None, ts, LANE), lambda i, n: (SLAB_QB + n, i, 0)),
                  pl.BlockSpec((None, 1, LANE), lambda i, n: (n, 0, 0)),
                  pl.BlockSpec((ts, LANE), lambda i, n: (i, 0)),
                  pl.BlockSpec((ts, LANE), lambda i, n: (i, 0))],
        out_specs=pl.BlockSpec((None, ts, LANE), lambda i, n: (n, i, 0)),
        out_shape=jax.ShapeDtypeStruct((nsl, s, LANE), BF16),
        compiler_params=_cparams(("parallel", "arbitrary"), est),
        name="gqa_qk_prep",
    )(z3, gains, cos, sin_signed)


def _softmax_update_t(k_tiles, vt_tiles, q_chunk, m_ref, acc_ref, bias_chunk=None, shift=None):
    n_chunks = m_ref.shape[1] // QCHUNK
    units = [(t, j) for t in range(len(k_tiles)) for j in range(n_chunks)]

    def scores(u):
        t, j = units[u]
        st = _qk_scores(k_tiles[t], q_chunk(j))
        return st if bias_chunk is None else st + bias_chunk(t, j)

    pending = [scores(u) for u in range(min(QK_AHEAD, len(units)))]
    for u, (t, j) in enumerate(units):
        cols = slice(j * QCHUNK, (j + 1) * QCHUNK)
        st = pending.pop(0)
        if u + QK_AHEAD < len(units):
            pending.append(scores(u + QK_AHEAD))
        m_prev = m_ref[:, cols]
        tile_max = jnp.max(st, axis=0, keepdims=True)
        if shift is None:
            m_new = jnp.maximum(m_prev, tile_max)
            sub = m_new
        else:
            m_new = jnp.maximum(m_prev, tile_max + shift)
            sub = m_new - shift
        alpha = jnp.exp2(m_prev - m_new)
        p = jnp.exp2(st - sub).astype(BF16)
        acc_ref[:, cols] = alpha * acc_ref[:, cols] + jnp.dot(vt_tiles[t], p, preferred_element_type=F32)
        m_ref[:, cols] = m_new


def _softmax_init(m_ref, acc_ref):
    m_ref[...] = jnp.full(m_ref.shape, -jnp.inf, F32)
    acc_ref[...] = jnp.zeros(acc_ref.shape, F32)


def _softmax_result_t(acc_ref):
    acc = acc_ref[...]
    return acc[0:HEAD_DIM] / acc[HEAD_DIM:HEAD_DIM + 1]


def _transposed_v(v, tk):
    n, s, _ = v.shape
    vt = jnp.swapaxes(v.reshape(n, s // tk, tk, HEAD_DIM), 2, 3)
    ones = jnp.ones((n, s // tk, VT_ROWS - HEAD_DIM, tk), BF16)
    return jnp.concatenate([vt, ones], axis=2)


def _qk_scores(a, b):
    return lax.dot_general(a, b, (((1,), (1,)), ((), ())), preferred_element_type=F32)


def _gqa_kernel(q_ref, k_ref, vt_ref, o_ref, m_ref, acc_ref):
    tq = q_ref.shape[1]
    n_k, _, tk = vt_ref.shape
    per_head = tq // QCHUNK

    def q_chunk(j):
        r0 = (j % per_head) * QCHUNK
        return q_ref[j // per_head, r0:r0 + QCHUNK, :]

    _softmax_init(m_ref, acc_ref)

    def key_tiles(kg, carry):
        tiles = [kg * KV_UNROLL + t for t in range(KV_UNROLL)]
        _softmax_update_t([k_ref[pl.ds(pl.multiple_of(ki * tk, tk), tk), :] for ki in tiles],
                          [vt_ref[ki] for ki in tiles], q_chunk, m_ref, acc_ref)
        return carry

    lax.fori_loop(0, n_k // KV_UNROLL, key_tiles, 0)
    ot = _softmax_result_t(acc_ref)
    for g in range(B_GROUP):
        o_ref[:, g * HEAD_DIM:(g + 1) * HEAD_DIM] = ot[:, g * tq:(g + 1) * tq].T.astype(o_ref.dtype)


def _gqa(qk, vt):
    s = qk.shape[1]
    tq = TQ_GQA
    n_k, _, tk = vt.shape[1:]
    assert n_k % KV_UNROLL == 0
    cols = B_GROUP * tq
    est = (2 * (2 * _nbytes((cols, LANE), BF16) + _nbytes((s, LANE), BF16) + _nbytes((VT_ROWS, s), BF16))
           + 3 * _nbytes((VT_ROWS, cols), F32) + 2 * (QK_AHEAD + 2) * _nbytes((tk, QCHUNK), F32))
    return pl.pallas_call(
        _gqa_kernel,
        grid=(B_KV_HEADS, s // tq),
        in_specs=[pl.BlockSpec((B_GROUP, tq, LANE), lambda h, i: (h, i, 0)),
                  pl.BlockSpec((None, s, LANE), lambda h, i: (B_HEADS + h, 0, 0)),
                  pl.BlockSpec((None, n_k, VT_ROWS, tk), lambda h, i: (h, 0, 0, 0))],
        out_specs=pl.BlockSpec((tq, B_GROUP * HEAD_DIM), lambda h, i: (i, h)),
        out_shape=jax.ShapeDtypeStruct((s, B_WIDTH), BF16),
        scratch_shapes=[pltpu.VMEM((1, cols), F32), pltpu.VMEM((VT_ROWS, cols), F32)],
        compiler_params=_cparams(("parallel", "parallel"), est),
        name="gqa_attention",
    )(qk, qk, vt)


def _t5_bias_block(t5_ref, h, rel0):
    shape = (BIAS_BLOCK, BIAS_BLOCK)
    rel = lax.broadcasted_iota(jnp.int32, shape, 0) - lax.broadcasted_iota(jnp.int32, shape, 1) + rel0
    nb = T5_BUCKETS // 2
    max_exact = nb // 2
    ret = jnp.where(rel > 0, nb, 0)
    n = jnp.abs(rel)
    nf = jnp.maximum(n, 1).astype(F32)
    large = max_exact + (jnp.log(nf / max_exact) / math.log(T5_MAX_DIST / max_exact)
                         * (nb - max_exact)).astype(jnp.int32)
    large = jnp.minimum(large, nb - 1)
    bucket = ret + jnp.where(n < max_exact, n, large)
    bias = jnp.full(shape, t5_ref[h], F32)
    for b in range(1, T5_BUCKETS):
        bias = jnp.where(bucket == b, t5_ref[b * A_HEADS + h], bias)
    return bias * LOG2E


def _t5_store_bias_tile(t5_ref, h, bias_ref, idx, key_offset, far_left, far_right):
    _, tk, tq = bias_ref.shape
    for rb in range(tk // BIAS_BLOCK):
        for cb in range(tq // BIAS_BLOCK):
            rel0 = key_offset + (rb - cb) * BIAS_BLOCK
            if rel0 - (BIAS_BLOCK - 1) >= T5_MAX_DIST:
                block = jnp.full((BIAS_BLOCK, BIAS_BLOCK), far_right, F32)
            elif rel0 + (BIAS_BLOCK - 1) <= -T5_MAX_DIST:
                block = jnp.full((BIAS_BLOCK, BIAS_BLOCK), far_left, F32)
            else:
                block = _t5_bias_block(t5_ref, h, rel0)
            bias_ref[idx, rb * BIAS_BLOCK:(rb + 1) * BIAS_BLOCK, cb * BIAS_BLOCK:(cb + 1) * BIAS_BLOCK] = block


def _diff_kernel(t5_ref, lam_ref, q_ref, k_ref, vt_ref, g_ref, o_ref, qs_ref, bias_ref, m_ref, acc_ref):
    h, qi = pl.program_id(0), pl.program_id(1)
    tq = q_ref.shape[0]
    n_k, _, tk = vt_ref.shape
    n_sub = tq // tk
    n_near = n_sub + 2
    per_map = tq // QCHUNK

    @pl.when(qi == 0)
    def _():
        far_left = t5_ref[(T5_BUCKETS // 2 - 1) * A_HEADS + h] * LOG2E
        far_right = t5_ref[(T5_BUCKETS - 1) * A_HEADS + h] * LOG2E
        bias_ref[0] = jnp.full((tk, tq), far_left, F32)
        for o in range(n_near):
            _t5_store_bias_tile(t5_ref, h, bias_ref, o + 1, (o - 1) * tk, far_left, far_right)
        bias_ref[n_near + 1] = jnp.full((tk, tq), far_right, F32)

    _softmax_init(m_ref, acc_ref)
    q = q_ref[...]
    lane = lax.broadcasted_iota(jnp.int32, q.shape, 1)
    zero = jnp.zeros_like(q)
    qs_ref[0:tq, :] = jnp.where(lane < A_QK_DIM, q, zero)
    qs_ref[tq:2 * tq, :] = jnp.where(lane >= A_QK_DIM, q, zero)

    def q_chunk(j):
        return qs_ref[j * QCHUNK:(j + 1) * QCHUNK, :]

    def tile_operands(kg):
        tiles = [kg * KV_UNROLL_DIFF + t for t in range(KV_UNROLL_DIFF)]
        return (tiles, [k_ref[pl.ds(pl.multiple_of(ki * tk, tk), tk), :] for ki in tiles],
                [vt_ref[ki] for ki in tiles])

    def near_tiles(kg, carry):
        tiles, k_tiles, vt_tiles = tile_operands(kg)
        bias_idx = [jnp.clip(ki - n_sub * qi + 2, 0, n_near + 1) for ki in tiles]

        def bias_chunk(t, j):
            c0 = (j % per_map) * QCHUNK
            return bias_ref[bias_idx[t], :, c0:c0 + QCHUNK]

        _softmax_update_t(k_tiles, vt_tiles, q_chunk, m_ref, acc_ref, bias_chunk=bias_chunk)
        return carry

    def far_tiles(kg, c):
        _, k_tiles, vt_tiles = tile_operands(kg)
        _softmax_update_t(k_tiles, vt_tiles, q_chunk, m_ref, acc_ref, shift=c)
        return c

    n_groups = n_k // KV_UNROLL_DIFF
    near_lo = jnp.maximum(n_sub * qi - 1, 0) // KV_UNROLL_DIFF
    near_hi = jnp.minimum((n_sub * qi + n_sub) // KV_UNROLL_DIFF + 1, n_groups)
    lax.fori_loop(0, near_lo, far_tiles, t5_ref[(T5_BUCKETS // 2 - 1) * A_HEADS + h] * LOG2E)
    lax.fori_loop(near_lo, near_hi, near_tiles, 0)
    lax.fori_loop(near_hi, n_groups, far_tiles, t5_ref[(T5_BUCKETS - 1) * A_HEADS + h] * LOG2E)
    ot = _softmax_result_t(acc_ref)
    out = (ot[:, 0:tq] - lam_ref[0] * ot[:, tq:2 * tq]).T
    o_ref[...] = (_rmsnorm_rows(out, g_ref[...]) * lam_ref[1]).astype(o_ref.dtype)


def _diff_attention(z3, vt, t5_flat, lam_pair, subln_g):
    s = z3.shape[1]
    tq = TQ_DIFF
    n_k, _, tk = vt.shape[1:]
    assert tk >= T5_MAX_DIST and tq % tk == 0 and s % tq == 0 and tq % QCHUNK == 0 and n_k % KV_UNROLL_DIFF == 0
    n_near = tq // tk + 2
    est = (2 * (2 * _nbytes((tq, LANE), BF16) + _nbytes((s, LANE), BF16) + _nbytes((VT_ROWS, s), BF16))
           + _nbytes((2 * tq, LANE), BF16) + (n_near + 2) * _nbytes((tk, tq), F32)
           + 3 * _nbytes((VT_ROWS, 2 * tq), F32) + 2 * (QK_AHEAD + 2) * _nbytes((tk, QCHUNK), F32))
    grid_spec = pltpu.PrefetchScalarGridSpec(
        num_scalar_prefetch=2,
        grid=(A_HEADS, s // tq),
        in_specs=[pl.BlockSpec((None, tq, LANE), lambda h, i, *_: (SLAB_QA + h, i, 0)),
                  pl.BlockSpec((None, s, LANE), lambda h, i, *_: (SLAB_KA + h, 0, 0)),
                  pl.BlockSpec((None, n_k, VT_ROWS, tk), lambda h, i, *_: (h, 0, 0, 0)),
                  pl.BlockSpec((1, LANE), lambda h, i, *_: (0, 0))],
        out_specs=pl.BlockSpec((tq, HEAD_DIM), lambda h, i, *_: (i, h)),
        scratch_shapes=[pltpu.VMEM((2 * tq, LANE), BF16), pltpu.VMEM((n_near + 2, tk, tq), F32),
                        pltpu.VMEM((1, 2 * tq), F32), pltpu.VMEM((VT_ROWS, 2 * tq), F32)])
    return pl.pallas_call(
        _diff_kernel,
        grid_spec=grid_spec,
        out_shape=jax.ShapeDtypeStruct((s, A_WIDTH), BF16),
        compiler_params=_cparams(("arbitrary", "arbitrary"), est),
        name="diff_attention",
    )(t5_flat, lam_pair, z3, z3, vt, subln_g.reshape(1, LANE))


def _na_window_start(first_row, rows):
    return max(0, min(first_row - NA_ROWS // 2, rows - NA_WIN_ROWS))


def _na_build_bias(rpb_ref, h, bias_ref, rows):
    cq = lax.broadcasted_iota(jnp.int32, (GRID_W, LANE), 0)
    ck = lax.broadcasted_iota(jnp.int32, (GRID_W, LANE), 1) % GRID_W
    cs = jnp.clip(cq - NA_COLS // 2, 0, GRID_W - NA_COLS)
    col_valid = (ck >= cs) & (ck < cs + NA_COLS)
    coff = jnp.clip(ck - cq + (NA_COLS - 1), 0, NA_COFF - 1)
    lower_half = lax.broadcasted_iota(jnp.int32, (GRID_W, LANE), 1) < GRID_W
    masked = jnp.full((GRID_W, LANE), MASK_VALUE, F32)
    col_bias = []
    for ro in range(NA_ROFF):
        base = (h * NA_ROFF + ro) * NA_COFF
        b = jnp.full((GRID_W, LANE), rpb_ref[base], F32)
        for c in range(1, NA_COFF):
            b = jnp.where(coff == c, rpb_ref[base + c], b)
        col_bias.append(jnp.where(col_valid, b * LOG2E, MASK_VALUE))
    n_groups = rows // NA_GROUP_ROWS
    for var, group in enumerate((0, 1, n_groups - 1)):
        ws = _na_window_start(group * NA_GROUP_ROWS, rows)
        for a in range(NA_GROUP_ROWS):
            r = group * NA_GROUP_ROWS + a
            rs = max(0, min(r - NA_ROWS // 2, rows - NA_ROWS))
            for ip in range(NA_WIN_ROWS // 2):
                halves = []
                for kr in (ws + 2 * ip, ws + 2 * ip + 1):
                    valid = rs <= kr < rs + NA_ROWS
                    halves.append(col_bias[kr - r + NA_ROWS - 1] if valid else masked)
                bias_ref[var, a * GRID_W:(a + 1) * GRID_W, ip * LANE:(ip + 1) * LANE] = (
                    jnp.where(lower_half, halves[0], halves[1]))


def _na_kernel(rpb_ref, q_ref, k_ref, v_ref, o_ref, bias_ref, *, rows):
    h, g = pl.program_id(0), pl.program_id(1)
    n_groups = rows // NA_GROUP_ROWS
    win = NA_WIN_ROWS * GRID_W

    @pl.when(g == 0)
    def _():
        _na_build_bias(rpb_ref, h, bias_ref, rows)

    ws = jnp.clip(g * NA_GROUP_ROWS - NA_ROWS // 2, 0, rows - NA_WIN_ROWS)
    start = pl.multiple_of(ws * GRID_W, (NA_ROWS // 2) * GRID_W)
    kw = k_ref[pl.ds(start, win), :]
    vw = v_ref[pl.ds(start, win), :]
    var = jnp.where(g == 0, 0, jnp.where(g == n_groups - 1, 2, 1))
    s = _qk_scores(q_ref[...], kw) + bias_ref[var]
    m = jnp.max(s, axis=-1, keepdims=True)
    p = jnp.exp2(s - m)
    l = jnp.sum(p, axis=-1, keepdims=True)
    o = jnp.dot(p.astype(BF16), vw, preferred_element_type=F32) / l
    o_ref[...] = o.astype(o_ref.dtype)


def _neighbourhood_attention(z3, rpb_flat):
    s = z3.shape[1]
    rows = s // GRID_W
    assert rows % NA_GROUP_ROWS == 0 and rows >= 3 * NA_GROUP_ROWS
    tq = NA_GROUP_ROWS * GRID_W
    win = NA_WIN_ROWS * GRID_W
    est = (2 * (2 * _nbytes((tq, LANE), BF16) + 2 * _nbytes((s, LANE), BF16))
           + 3 * _nbytes((tq, win), F32) + 3 * _nbytes((tq, win), F32))
    grid_spec = pltpu.PrefetchScalarGridSpec(
        num_scalar_prefetch=1,
        grid=(C_HEADS, rows // NA_GROUP_ROWS),
        in_specs=[pl.BlockSpec((None, tq, LANE), lambda h, g, *_: (SLAB_QC + h, g, 0)),
                  pl.BlockSpec((None, s, LANE), lambda h, g, *_: (SLAB_KC + h, 0, 0)),
                  pl.BlockSpec((None, s, LANE), lambda h, g, *_: (SLAB_VC + h, 0, 0))],
        out_specs=pl.BlockSpec((tq, HEAD_DIM), lambda h, g, *_: (g, h)),
        scratch_shapes=[pltpu.VMEM((3, tq, win), F32)])
    return pl.pallas_call(
        functools.partial(_na_kernel, rows=rows),
        grid_spec=grid_spec,
        out_shape=jax.ShapeDtypeStruct((s, C_WIDTH), BF16),
        compiler_params=_cparams(("arbitrary", "arbitrary"), est),
        name="neighbourhood_attention",
    )(rpb_flat, z3, z3, z3)


def _router_kernel(x_ref, g_ref, wr_ref, o_ref):
    h = _rmsnorm_rows(x_ref[...], g_ref[...])
    logits = jnp.dot(h, wr_ref[...], preferred_element_type=F32, precision=lax.Precision.HIGHEST)
    lane = lax.broadcasted_iota(jnp.int32, logits.shape, 1)
    lane_f = lane.astype(F32)
    logits = jnp.where(lane < N_EXPERTS, logits, -jnp.inf)
    m1 = jnp.max(logits, axis=-1, keepdims=True)
    i1 = jnp.min(jnp.where(logits == m1, lane_f, float(LANE)), axis=-1, keepdims=True)
    rest = jnp.where(lane_f == i1, -jnp.inf, logits)
    m2 = jnp.max(rest, axis=-1, keepdims=True)
    i2 = jnp.min(jnp.where(rest == m2, lane_f, float(LANE)), axis=-1, keepdims=True)
    e = jnp.exp(m2 - m1)
    g1 = 1.0 / (1.0 + e)
    g2 = e / (1.0 + e)
    o_ref[...] = jnp.where(lane == 0, i1, jnp.where(lane == 1, i2, jnp.where(lane == 2, g1, g2)))


def _router(x, g, w_router):
    s, d = x.shape
    tm = TM_ROUTE
    wr = jnp.zeros((d, LANE), F32).at[:, :N_EXPERTS].set(w_router)
    est = 2 * (_nbytes((tm, d), F32) + _nbytes((d, LANE), F32) + _nbytes((tm, LANE), F32)) + 4 * _nbytes((tm, d), F32)
    return pl.pallas_call(
        _router_kernel,
        grid=(s // tm,),
        in_specs=[pl.BlockSpec((tm, d), lambda i: (i, 0)),
                  pl.BlockSpec((1, d), lambda i: (0, 0)),
                  pl.BlockSpec((d, LANE), lambda i: (0, 0))],
        out_specs=pl.BlockSpec((tm, LANE), lambda i: (i, 0)),
        out_shape=jax.ShapeDtypeStruct((s, LANE), F32),
        compiler_params=_cparams(("parallel",), est),
        name="moe_router",
    )(x, g.reshape(1, d), wr)


def _row_gather(src_hbm, dst_ref, sem, n_rows, row_of):
    def issue(r, carry):
        pltpu.make_async_copy(src_hbm.at[pl.ds(row_of(r), 1)], dst_ref.at[pl.ds(r, 1)], sem).start()
        return carry
    lax.fori_loop(0, n_rows, issue, 0, unroll=DMA_ISSUE_UNROLL)
    pltpu.make_async_copy(src_hbm.at[pl.ds(0, n_rows)], dst_ref.at[pl.ds(0, n_rows)], sem).wait()


def _moe_kernel(tile_e_ref, nused_ref, tok_ref, x_hbm, gn_ref, wg_ref, wu_ref, wd_ref,
                o_ref, xg_ref, xb_ref, acc_ref, sem):
    i, f = pl.program_id(0), pl.program_id(1)
    tm = xg_ref.shape[0]
    used = i < nused_ref[0]
    last = f == pl.num_programs(1) - 1

    @pl.when(used & (f == 0))
    def _():
        _row_gather(x_hbm, xg_ref, sem, tm, lambda r: tok_ref[i * tm + r])
        xb_ref[...] = _rmsnorm_rows(xg_ref[...], gn_ref[...]).astype(BF16)
        acc_ref[...] = jnp.zeros(acc_ref.shape, F32)

    @pl.when(used)
    def _():
        hb = xb_ref[...]
        g = jnp.dot(hb, wg_ref[...], preferred_element_type=F32)
        u = jnp.dot(hb, wu_ref[...], preferred_element_type=F32)
        act = (g * _sigmoid(g) * u).astype(BF16)
        acc_ref[...] += jnp.dot(act, wd_ref[...], preferred_element_type=F32)

    @pl.when(used & last)
    def _():
        o_ref[...] = acc_ref[...]

    @pl.when(jnp.logical_not(used) & last)
    def _():
        o_ref[...] = jnp.zeros(o_ref.shape, F32)


def _moe_experts(x, gn, wg, wu, wd, tile_e, n_used, buf_tok):
    d = x.shape[1]
    fdim = wg.shape[2]
    cap = buf_tok.shape[0]
    tm, tf = TM_MOE, TF_MOE
    nf = fdim // tf

    def fsel(i, f, nused):
        return jnp.where(i < nused[0], f, nf - 1)

    est = (2 * (3 * _nbytes((d, tf), BF16) + _nbytes((tm, d), F32))
           + 2 * _nbytes((tm, d), F32) + _nbytes((tm, d), BF16)
           + 3 * _nbytes((tm, tf), F32) + _nbytes((tm, d), F32))
    grid_spec = pltpu.PrefetchScalarGridSpec(
        num_scalar_prefetch=3,
        grid=(cap // tm, nf),
        in_specs=[pl.BlockSpec(memory_space=pl.ANY),
                  pl.BlockSpec((1, d), lambda i, f, te, nu, tok: (0, 0)),
                  pl.BlockSpec((None, d, tf), lambda i, f, te, nu, tok: (te[i], 0, fsel(i, f, nu))),
                  pl.BlockSpec((None, d, tf), lambda i, f, te, nu, tok: (te[i], 0, fsel(i, f, nu))),
                  pl.BlockSpec((None, tf, d), lambda i, f, te, nu, tok: (te[i], fsel(i, f, nu), 0))],
        out_specs=pl.BlockSpec((tm, d), lambda i, f, te, nu, tok: (i, 0)),
        scratch_shapes=[pltpu.VMEM((tm, d), F32), pltpu.VMEM((tm, d), BF16),
                        pltpu.VMEM((tm, d), F32), pltpu.SemaphoreType.DMA(())])
    return pl.pallas_call(
        _moe_kernel,
        grid_spec=grid_spec,
        out_shape=jax.ShapeDtypeStruct((cap, d), F32),
        compiler_params=_cparams(("arbitrary", "arbitrary"), est),
        name="moe_experts",
    )(tile_e, n_used, buf_tok, x, gn.reshape(1, d), wg, wu, wd)


def _combine_kernel(pos_ref, x_ref, route_ref, y_hbm, gf_ref, o_ref, buf_ref, sem, *, final_norm):
    i = pl.program_id(0)
    tm = x_ref.shape[0]
    slot = i % 2

    def start_gather(step, dst_slot):
        def issue(r, carry):
            row = pos_ref[step * (TOP_K * tm) + r]
            pltpu.make_async_copy(y_hbm.at[pl.ds(row, 1)], buf_ref.at[dst_slot, pl.ds(r, 1)],
                                  sem.at[dst_slot]).start()
            return carry
        lax.fori_loop(0, TOP_K * tm, issue, 0, unroll=DMA_ISSUE_UNROLL)

    @pl.when(i == 0)
    def _():
        start_gather(0, 0)

    @pl.when(i + 1 < pl.num_programs(0))
    def _():
        start_gather(i + 1, 1 - slot)

    pltpu.make_async_copy(y_hbm.at[pl.ds(0, TOP_K * tm)], buf_ref.at[slot], sem.at[slot]).wait()
    route = route_ref[...]
    y = (route[:, TOP_K:TOP_K + 1] * buf_ref[slot, 0:tm, :]
         + route[:, TOP_K + 1:TOP_K + 2] * buf_ref[slot, tm:2 * tm, :])
    out = x_ref[...] + y
    if final_norm:
        out = _rmsnorm_rows(out, gf_ref[...])
    o_ref[...] = out


def _moe_combine(x, yb, pos, route, final_gain):
    s, d = x.shape
    tm = TM_COMB
    pos_flat = jnp.swapaxes(pos.reshape(s // tm, tm, TOP_K), 1, 2).reshape(-1)
    final_norm = final_gain is not None
    gf = final_gain if final_norm else jnp.ones((d,), F32)
    est = 2 * (2 * _nbytes((tm, d), F32) + _nbytes((tm, LANE), F32)) + 2 * _nbytes((TOP_K * tm, d), F32) \
        + 3 * _nbytes((tm, d), F32)
    grid_spec = pltpu.PrefetchScalarGridSpec(
        num_scalar_prefetch=1,
        grid=(s // tm,),
        in_specs=[pl.BlockSpec((tm, d), lambda i, pos: (i, 0)),
                  pl.BlockSpec((tm, LANE), lambda i, pos: (i, 0)),
                  pl.BlockSpec(memory_space=pl.ANY),
                  pl.BlockSpec((1, d), lambda i, pos: (0, 0))],
        out_specs=pl.BlockSpec((tm, d), lambda i, pos: (i, 0)),
        scratch_shapes=[pltpu.VMEM((2, TOP_K * tm, d), F32), pltpu.SemaphoreType.DMA((2,))])
    return pl.pallas_call(
        functools.partial(_combine_kernel, final_norm=final_norm),
        grid_spec=grid_spec,
        out_shape=jax.ShapeDtypeStruct((s, d), F32),
        compiler_params=_cparams(("arbitrary",), est),
        name="moe_combine",
    )(pos_flat, x, route, yb, gf.reshape(1, d))


def _moe_layer(x, gn, w_router, wg, wu, wd, final_gain=None):
    s = x.shape[0]
    tm = TM_MOE
    route = _router(x, gn, w_router)
    e_flat = route[:, 0:TOP_K].astype(jnp.int32).reshape(-1)
    n_assign = s * TOP_K
    onehot = (e_flat[:, None] == jnp.arange(N_EXPERTS, dtype=jnp.int32)[None, :]).astype(jnp.int32)
    csum = jnp.cumsum(onehot, axis=0)
    rank = jnp.sum(csum * onehot, axis=1) - 1
    counts = csum[-1]
    pcounts = (counts + tm - 1) // tm * tm
    pends = jnp.cumsum(pcounts)
    pstarts = pends - pcounts
    dest = jnp.sum(pstarts[None, :] * onehot, axis=1) + rank
    cap = n_assign + N_EXPERTS * tm
    tok_flat = jnp.arange(n_assign, dtype=jnp.int32) // TOP_K
    buf_tok = jnp.zeros((cap,), jnp.int32).at[dest].set(tok_flat)
    n_tiles = cap // tm
    tile_e = jnp.minimum(jnp.searchsorted(pends, jnp.arange(n_tiles, dtype=jnp.int32) * tm, side='right'),
                         N_EXPERTS - 1).astype(jnp.int32)
    n_used = (pends[-1:] // tm).astype(jnp.int32)
    yb = _moe_experts(x, gn, wg, wu, wd, tile_e, n_used, buf_tok)
    return _moe_combine(x, yb, dest.astype(jnp.int32).reshape(s, TOP_K), route, final_gain)


def kernel(x, w_in, w_branch_a, w_branch_b, w_branch_c, w_out, norm_mix, norm_ffn, norm_final, t5_bias, diff_lambda, diff_subln, qk_norm_b, na_rpb, ffn_gate, ffn_up, ffn_down, moe_router, moe_gate, moe_up, moe_down):
    bn, s, d = x.shape
    assert bn == 1 and d == D_MODEL
    depth = w_in.shape[0]
    xs = x.reshape(s, d)
    cos, sin_signed = _rope_tables(s)
    t5_flat = t5_bias.astype(F32).reshape(-1)
    colscale = jnp.ones((1, QKV_WIDTH), F32)
    colscale = colscale.at[:, SLAB_QA * LANE:SLAB_KA * LANE].set(A_QK_DIM ** -0.5 * LOG2E)
    colscale = colscale.at[:, SLAB_QC * LANE:SLAB_KC * LANE].set(HEAD_DIM ** -0.5 * LOG2E)

    for l in range(depth):
        h = _rmsnorm(xs, norm_mix[l], BF16)
        w_l = w_in[l].astype(BF16)
        z3 = _proj_slabs(h, w_l, colscale)
        gates = _matmul_sigmoid(h, w_l, QKV_WIDTH)

        lam_init = 0.8 - 0.6 * math.exp(-0.3 * l)
        lp = diff_lambda[l].astype(F32)
        lam = jnp.exp(jnp.sum(lp[0] * lp[1])) - jnp.exp(jnp.sum(lp[2] * lp[3])) + lam_init
        lam_pair = jnp.stack([lam, jnp.asarray(1.0 - lam_init, F32)]).astype(F32)
        vt_a = _transposed_v(z3[SLAB_VA:SLAB_VA + A_HEADS], TK_DIFF)
        vt_b = _transposed_v(z3[SLAB_VB:SLAB_VB + B_KV_HEADS], TK_GQA)
        ya = _diff_attention(z3, vt_a, t5_flat, lam_pair, diff_subln[l].astype(F32))

        q_gain = qk_norm_b[l, 0].astype(F32) * (HEAD_DIM ** -0.5 * LOG2E)
        k_gain = qk_norm_b[l, 1].astype(F32)
        gains = jnp.concatenate([jnp.tile(q_gain[None], (B_HEADS, 1)),
                                 jnp.tile(k_gain[None], (B_KV_HEADS, 1))])[:, None, :]
        yb = _gqa(_qk_prep(z3, gains, cos, sin_signed), vt_b)

        yc = _neighbourhood_attention(z3, na_rpb[l].astype(F32).reshape(-1))

        merged = _merge(ya, yb, yc, w_branch_a[l].astype(BF16), w_branch_b[l].astype(BF16),
                        w_branch_c[l].astype(BF16), gates)
        xs = _matmul_residual(merged, w_out[l].astype(BF16), xs, TM_OUT, TN_OUT, "out_proj")

        j = l // 2
        if l % 2 == 0:
            hf = _rmsnorm(xs, norm_ffn[l], BF16)
            act = _glu(hf, ffn_gate[j].astype(BF16), ffn_up[j].astype(BF16))
            xs = _matmul_residual(act, ffn_down[j].astype(BF16), xs, TM_DOWN, TN_DOWN, "ffn_down")
        else:
            last = l == depth - 1
            xs = _moe_layer(xs, norm_ffn[l], moe_router[j], moe_gate[j].astype(BF16),
                            moe_up[j].astype(BF16), moe_down[j].astype(BF16),
                            final_gain=norm_final.astype(F32) if last else None)
            if last:
                return xs.astype(x.dtype).reshape(bn, s, d)
    return _rmsnorm(xs, norm_final, x.dtype).reshape(bn, s, d)
```
